```python
import jax, jax.numpy as jnp
from jax import lax
import numpy as np

D_MODEL = 1024
BATCH = 2
SEQ = 16384
DEPTH = 4
DEC_BATCH = 8
DEC_SEQ = 16
PAST_LEN = 4096

CHUNK = 64
QBLOCK = 128
EPS = 1e-6
FOX_HEADS = 8
FOX_DIM = 64
FORGET_BIAS_INIT = 2.0
MLA_HEADS = 8
MLA_Q_RANK = 256
MLA_KV_RANK = 128
MLA_NOPE = 64
MLA_ROPE = 32
MLA_V = 64
MLA_THETA = 10000.0
DSA_HEADS = 8
DSA_KV_HEADS = 2
DSA_DIM = 64
IDX_HEADS = 8
IDX_DIM = 64
TOPK_MAX = 256
ROPE_THETA = 500000.0
DSA_ROT = DSA_DIM // 4
IDX_ROT = IDX_DIM // 4
BRANCH_W = FOX_HEADS * FOX_DIM
N_BRANCH = 3
D_FF = -(-8 * D_MODEL // (3 * 256)) * 256
IN_SIZES = (FOX_HEADS * FOX_DIM, FOX_HEADS * FOX_DIM, FOX_HEADS * FOX_DIM, FOX_HEADS,
            MLA_Q_RANK, MLA_KV_RANK, MLA_ROPE,
            DSA_HEADS * DSA_DIM, DSA_KV_HEADS * DSA_DIM, DSA_KV_HEADS * DSA_DIM,
            IDX_HEADS * IDX_DIM, IDX_DIM, IDX_HEADS,
            N_BRANCH * D_MODEL)
N_IN = sum(IN_SIZES)

kernel_name = "hybrid_fox_mla_dsa_streaming_step"


def _rmsnorm(x, g):
    x32 = x.astype(jnp.float32)
    y = x32 * lax.rsqrt(jnp.mean(x32 * x32, axis=-1, keepdims=True) + EPS)
    return (y * g.astype(jnp.float32)).astype(x.dtype)


def _rope(x, pos, rot_dim, theta):
    half = rot_dim // 2
    inv_freq = theta ** (-jnp.arange(half, dtype=jnp.float32) * (2.0 / rot_dim))
    ang = pos.astype(jnp.float32)[:, None] * inv_freq[None, :]
    cos = jnp.cos(ang)[None, :, None, :]
    sin = jnp.sin(ang)[None, :, None, :]
    xr = x[..., :rot_dim].astype(jnp.float32)
    x1, x2 = xr[..., :half], xr[..., half:]
    out = jnp.concatenate([x1 * cos - x2 * sin, x2 * cos + x1 * sin], axis=-1).astype(x.dtype)
    return jnp.concatenate([out, x[..., rot_dim:]], axis=-1)


def _split_cols(z):
    idx = [int(i) for i in np.cumsum(IN_SIZES)[:-1]]
    return jnp.split(z, idx, axis=-1)


def _over_query_blocks(fn, q_pos, *q_arrays):
    T = q_pos.shape[0]
    if T <= QBLOCK or T % QBLOCK:
        return fn(q_pos, *q_arrays)
    nb = T // QBLOCK
    pos_b = q_pos.reshape(nb, QBLOCK)
    arrs_b = tuple(a.reshape(a.shape[0], nb, QBLOCK, *a.shape[2:]).swapaxes(0, 1) for a in q_arrays)
    out = lax.map(lambda xs: fn(xs[0], *xs[1]), (pos_b, arrs_b))
    return out.swapaxes(0, 1).reshape(out.shape[1], T, *out.shape[3:])


def _dense_attention(q, k, v, q_pos, k_pos, scale, per_frame, q_bias=None, k_bias=None):
    kb_t = None if k_bias is None else k_bias.transpose(0, 2, 1)

    def block(qp, qb, *qbias):
        s = jnp.einsum('bqhd,bkhd->bhqk', qb, k).astype(jnp.float32) * scale
        if qbias:
            s = s + qbias[0].transpose(0, 2, 1)[..., None] - kb_t[:, :, None, :]
        if per_frame:
            allowed = k_pos[None, :] <= qp[:, None]
        else:
            allowed = (k_pos[None, :] // CHUNK) <= (qp[:, None] // CHUNK)
        s = jnp.where(allowed[None, None], s, -jnp.inf)
        p = jax.nn.softmax(s, axis=-1)
        return jnp.einsum('bhqk,bkhd->bqhd', p.astype(v.dtype), v)

    extra = () if q_bias is None else (q_bias,)
    return _over_query_blocks(block, q_pos, q, *extra)


def _sparse_attention(q, k, v, iq, ik, iw, q_pos, k_pos):
    n_sel = min(TOPK_MAX, k.shape[1] // 4)
    rep = DSA_HEADS // DSA_KV_HEADS

    def block(qp, qb, iqb, iwb):
        B, Tq = qb.shape[:2]
        sc = jax.nn.relu(jnp.einsum('bqhd,bkd->bqhk', iqb, ik).astype(jnp.float32))
        sc = jnp.einsum('bqhk,bqh->bqk', sc, iwb.astype(jnp.float32))
        admissible = (k_pos[None, :] // CHUNK) <= (qp[:, None] // CHUNK)
        sc = jnp.where(admissible[None], sc, -jnp.inf)
        top_val, top_idx = lax.top_k(sc, n_sel)
        valid = jnp.isfinite(top_val)
        kg = jax.vmap(lambda kk, ii: kk[ii])(k, top_idx)
        vg = jax.vmap(lambda vv, ii: vv[ii])(v, top_idx)
        qg = qb.reshape(B, Tq, DSA_KV_HEADS, rep, DSA_DIM)
        s = jnp.einsum('bqgrd,bqkgd->bqgrk', qg, kg).astype(jnp.float32) * (DSA_DIM ** -0.5)
        s = jnp.where(valid[:, :, None, None, :], s, -jnp.inf)
        p = jax.nn.softmax(s, axis=-1)
        o = jnp.einsum('bqgrk,bqkgd->bqgrd', p.astype(vg.dtype), vg)
        return o.reshape(B, Tq, DSA_HEADS, DSA_DIM)

    return _over_query_blocks(block, q_pos, q, iq, iw)


def _layer(x, pos, past, norm_attn, w_in, b_forget, norm_mla_q, norm_mla_kv, w_mla_uq, w_mla_ukv,
           w_br_fox, w_br_mla, w_br_dsa, w_out, norm_ffn, w_ffn_gate, w_ffn_up, w_ffn_down):
    B, T, _ = x.shape
    h = _rmsnorm(x, norm_attn)
    (fq, fk, fv, ff, cq, ckv, kr, dq, dk, dv, iq, ik, iw, gz) = _split_cols(h @ w_in)
    fq = fq.reshape(B, T, FOX_HEADS, FOX_DIM)
    fk = fk.reshape(B, T, FOX_HEADS, FOX_DIM)
    fv = fv.reshape(B, T, FOX_HEADS, FOX_DIM)
    logf = jax.nn.log_sigmoid(ff.astype(jnp.float32) + b_forget.astype(jnp.float32))
    qm = (_rmsnorm(cq, norm_mla_q) @ w_mla_uq).reshape(B, T, MLA_HEADS, MLA_NOPE + MLA_ROPE)
    q_nope, q_pe = qm[..., :MLA_NOPE], qm[..., MLA_NOPE:]
    q_pe = _rope(q_pe, pos, MLA_ROPE, MLA_THETA)
    ckv = _rmsnorm(ckv, norm_mla_kv)
    kpe = _rope(kr[:, :, None, :], pos, MLA_ROPE, MLA_THETA)[:, :, 0]
    dq = _rope(dq.reshape(B, T, DSA_HEADS, DSA_DIM), pos, DSA_ROT, ROPE_THETA)
    dk = _rope(dk.reshape(B, T, DSA_KV_HEADS, DSA_DIM), pos, DSA_ROT, ROPE_THETA)
    dv = dv.reshape(B, T, DSA_KV_HEADS, DSA_DIM)
    iq = _rope(iq.reshape(B, T, IDX_HEADS, IDX_DIM), pos, IDX_ROT, ROPE_THETA)
    ik = _rope(ik[:, :, None, :], pos, IDX_ROT, ROPE_THETA)[:, :, 0]

    new_state = (fk, fv, logf, ckv, kpe, dk, dv, ik)
    if past is None:
        full = new_state
    else:
        full = tuple(jnp.concatenate([p.astype(n.dtype), n], axis=1) for p, n in zip(past, new_state))
    fk_all, fv_all, logf_all, ckv_all, kpe_all, dk_all, dv_all, ik_all = full
    L = fk_all.shape[1]
    k_pos = jnp.arange(L)

    cum = jnp.cumsum(logf_all, axis=1)
    o_a = _dense_attention(fq, fk_all, fv_all, pos, k_pos, FOX_DIM ** -0.5, True,
                           q_bias=cum[:, L - T:], k_bias=cum)
    kv = (ckv_all @ w_mla_ukv).reshape(B, L, MLA_HEADS, MLA_NOPE + MLA_V)
    k_m = jnp.concatenate([kv[..., :MLA_NOPE],
                           jnp.broadcast_to(kpe_all[:, :, None, :], (B, L, MLA_HEADS, MLA_ROPE))], axis=-1)
    q_m = jnp.concatenate([q_nope, q_pe], axis=-1)
    o_b = _dense_attention(q_m, k_m, kv[..., MLA_NOPE:], pos, k_pos, (MLA_NOPE + MLA_ROPE) ** -0.5, False)
    o_c = _sparse_attention(dq, dk_all, dv_all, iq, ik_all, iw, pos, k_pos)

    gates = jax.nn.sigmoid(gz).reshape(B, T, N_BRANCH, D_MODEL)
    merged = (gates[:, :, 0] * (o_a.reshape(B, T, BRANCH_W) @ w_br_fox)
              + gates[:, :, 1] * (o_b.reshape(B, T, BRANCH_W) @ w_br_mla)
              + gates[:, :, 2] * (o_c.reshape(B, T, BRANCH_W) @ w_br_dsa))
    x = x + merged @ w_out
    h2 = _rmsnorm(x, norm_ffn)
    x = x + (jax.nn.silu(h2 @ w_ffn_gate) * (h2 @ w_ffn_up)) @ w_ffn_down
    return x, new_state


def setup_inputs(seed: int = 0) -> dict:
    key = jax.random.key(seed)
    keys = jax.random.split(key, 32)
    counter = [0]

    def nrm(shape, scale=1.0):
        sub = keys[counter[0]]
        counter[0] += 1
        return scale * jax.random.normal(sub, shape, jnp.float32)

    d = {}
    d['x_prompt'] = nrm((BATCH, SEQ, D_MODEL))
    d['x_sample'] = nrm((DEC_BATCH, DEC_SEQ, D_MODEL))
    d['cache_fox_k'] = nrm((DEPTH, DEC_BATCH, PAST_LEN, FOX_HEADS, FOX_DIM))
    d['cache_fox_v'] = nrm((DEPTH, DEC_BATCH, PAST_LEN, FOX_HEADS, FOX_DIM))
    d['cache_fox_logf'] = jax.nn.log_sigmoid(FORGET_BIAS_INIT + nrm((DEPTH, DEC_BATCH, PAST_LEN, FOX_HEADS)))
    d['cache_mla_ckv'] = nrm((DEPTH, DEC_BATCH, PAST_LEN, MLA_KV_RANK))
    d['cache_mla_kpe'] = nrm((DEPTH, DEC_BATCH, PAST_LEN, MLA_ROPE))
    d['cache_dsa_k'] = nrm((DEPTH, DEC_BATCH, PAST_LEN, DSA_KV_HEADS, DSA_DIM))
    d['cache_dsa_v'] = nrm((DEPTH, DEC_BATCH, PAST_LEN, DSA_KV_HEADS, DSA_DIM))
    d['cache_dsa_idxk'] = nrm((DEPTH, DEC_BATCH, PAST_LEN, IDX_DIM))
    d['norm_attn'] = 1.0 + nrm((DEPTH, D_MODEL), 0.02)
    d['w_in'] = nrm((DEPTH, D_MODEL, N_IN), D_MODEL ** -0.5)
    d['b_forget'] = FORGET_BIAS_INIT + nrm((DEPTH, FOX_HEADS), 0.1)
    d['norm_mla_q'] = 1.0 + nrm((DEPTH, MLA_Q_RANK), 0.02)
    d['norm_mla_kv'] = 1.0 + nrm((DEPTH, MLA_KV_RANK), 0.02)
    d['w_mla_uq'] = nrm((DEPTH, MLA_Q_RANK, MLA_HEADS * (MLA_NOPE + MLA_ROPE)), MLA_Q_RANK ** -0.5)
    d['w_mla_ukv'] = nrm((DEPTH, MLA_KV_RANK, MLA_HEADS * (MLA_NOPE + MLA_V)), MLA_KV_RANK ** -0.5)
    d['w_br_fox'] = nrm((DEPTH, BRANCH_W, D_MODEL), BRANCH_W ** -0.5)
    d['w_br_mla'] = nrm((DEPTH, BRANCH_W, D_MODEL), BRANCH_W ** -0.5)
    d['w_br_dsa'] = nrm((DEPTH, BRANCH_W, D_MODEL), BRANCH_W ** -0.5)
    d['w_out'] = nrm((DEPTH, D_MODEL, D_MODEL), D_MODEL ** -0.5)
    d['norm_ffn'] = 1.0 + nrm((DEPTH, D_MODEL), 0.02)
    d['w_ffn_gate'] = nrm((DEPTH, D_MODEL, D_FF), D_MODEL ** -0.5)
    d['w_ffn_up'] = nrm((DEPTH, D_MODEL, D_FF), D_MODEL ** -0.5)
    d['w_ffn_down'] = nrm((DEPTH, D_FF, D_MODEL), D_FF ** -0.5)
    d['norm_final'] = 1.0 + nrm((D_MODEL,), 0.02)
    return d


def reference(x_prompt, x_sample, cache_fox_k, cache_fox_v, cache_fox_logf, cache_mla_ckv, cache_mla_kpe,
              cache_dsa_k, cache_dsa_v, cache_dsa_idxk, norm_attn, w_in, b_forget, norm_mla_q, norm_mla_kv,
              w_mla_uq, w_mla_ukv, w_br_fox, w_br_mla, w_br_dsa, w_out, norm_ffn, w_ffn_gate, w_ffn_up,
              w_ffn_down, norm_final):
    weights = (norm_attn, w_in, b_forget, norm_mla_q, norm_mla_kv, w_mla_uq, w_mla_ukv,
               w_br_fox, w_br_mla, w_br_dsa, w_out, norm_ffn, w_ffn_gate, w_ffn_up, w_ffn_down)
    caches = (cache_fox_k, cache_fox_v, cache_fox_logf, cache_mla_ckv, cache_mla_kpe,
              cache_dsa_k, cache_dsa_v, cache_dsa_idxk)
    pos_p = jnp.arange(x_prompt.shape[1])
    pos_s = cache_fox_k.shape[2] + jnp.arange(x_sample.shape[1])
    xp, xs = x_prompt, x_sample
    states_p, states_s = [], []
    for l in range(DEPTH):
        params = [w[l] for w in weights]
        xp, st_p = _layer(xp, pos_p, None, *params)
        xs, st_s = _layer(xs, pos_s, tuple(c[l] for c in caches), *params)
        states_p.append(st_p)
        states_s.append(st_s)
    (fox_k_p, fox_v_p, fox_logf_p, mla_ckv_p, mla_kpe_p, dsa_k_p, dsa_v_p, dsa_idxk_p) = [
        jnp.stack(a, axis=0) for a in zip(*states_p)]
    (fox_k_s, fox_v_s, fox_logf_s, mla_ckv_s, mla_kpe_s, dsa_k_s, dsa_v_s, dsa_idxk_s) = [
        jnp.stack(a, axis=0) for a in zip(*states_s)]
    y_prompt = _rmsnorm(xp, norm_final)
    y_sample = _rmsnorm(xs, norm_final)
    return (y_prompt, y_sample, fox_k_p, fox_k_s, fox_v_p, fox_v_s, fox_logf_p, fox_logf_s,
            mla_ckv_p, mla_ckv_s, mla_kpe_p, mla_kpe_s, dsa_k_p, dsa_k_s, dsa_v_p, dsa_v_s,
            dsa_idxk_p, dsa_idxk_s)
```

```python
import functools

import numpy as np
import jax
import jax.numpy as jnp
from jax import lax
from jax.experimental import pallas as pl
from jax.experimental.pallas import tpu as pltpu

F32 = jnp.float32
BF16 = jnp.bfloat16

CHUNK = 64
EPS = 1e-6
HEADS = 8
HEAD_DIM = 64
MLA_ROPE = 32
MLA_THETA = 10000.0
DSA_KV_HEADS = 2
ROPE_THETA = 500000.0
PART_ROT = HEAD_DIM // 4
TOPK_MAX = 256
BRANCH_W = HEADS * HEAD_DIM

LANES = 128
VMEM_LIMIT = 56 * 1024 * 1024
NEG = -1e30

C_GZ = 0
C_FQ = 3072
C_FK = 3584
C_FV = 4096
C_DQ = 4608
C_IQ = 5120
C_CQ = 5632
C_CKV = 5888
C_DK = 6016
C_DV = 6144
C_MISC = 6272
N_PACK = 6400
M_IK, M_KR, M_FF, M_IW = 0, 64, 96, 104

KEY_NEG_INF = int(np.array(0xFF800000 ^ 0x7FFFFFFF, dtype=np.uint32).view(np.int32))
KEY_POS_INF = 0x7F800000
INT_MIN = -(2 ** 31)
INT_MAX = 2 ** 31 - 1


def _cparams(*sem):
    return pltpu.CompilerParams(dimension_semantics=sem, vmem_limit_bytes=VMEM_LIMIT)


def _round_up(a, b):
    return -(-a // b) * b


def _row_tile(m, cap=512):
    t = cap
    while t > 8 and m % t:
        t //= 2
    assert m % t == 0
    return t


def _rms_mm_kernel(x_ref, g_ref, w_ref, o_ref, xn_ref, *, normalize):
    @pl.when(pl.program_id(1) == 0)
    def _():
        x = x_ref[...].astype(F32)
        if normalize:
            x = x * lax.rsqrt(jnp.mean(x * x, axis=-1, keepdims=True) + EPS) * g_ref[...]
        xn_ref[...] = x.astype(BF16)

    o_ref[...] = jnp.dot(xn_ref[...], w_ref[...], preferred_element_type=F32).astype(o_ref.dtype)


def _rms_matmul(x, col_block, k, g, w, *, tn, out_dtype=F32):
    m = x.shape[0]
    n = w.shape[1]
    tm = _row_tile(m)
    normalize = g is not None
    if g is None:
        g = jnp.ones((k,), F32)
    return pl.pallas_call(
        functools.partial(_rms_mm_kernel, normalize=normalize),
        grid=(m // tm, n // tn),
        in_specs=[pl.BlockSpec((tm, k), lambda i, j: (i, col_block)),
                  pl.BlockSpec((1, k), lambda i, j: (0, 0)),
                  pl.BlockSpec((k, tn), lambda i, j: (0, j))],
        out_specs=pl.BlockSpec((tm, tn), lambda i, j: (i, j)),
        out_shape=jax.ShapeDtypeStruct((m, n), out_dtype),
        scratch_shapes=[pltpu.VMEM((tm, k), BF16)],
        compiler_params=_cparams("parallel", "arbitrary"),
        name="rms_matmul",
    )(x, g.reshape(1, k).astype(F32), w)


def _rope_tables(pos):
    lane = np.arange(LANES)

    def pattern(theta, rot, period, lane0, width):
        half = rot // 2
        inv_freq = theta ** (-jnp.arange(half, dtype=F32) * (2.0 / rot))
        ang = pos.astype(F32)[:, None] * inv_freq[None, :]
        cos, sin = jnp.cos(ang), jnp.sin(ang)
        inside = (lane >= lane0) & (lane < lane0 + width)
        r = (lane - lane0) % period
        first = inside & (r < half)
        second = inside & (r >= half) & (r < rot)
        fidx = np.where(first, r, np.where(second, r - half, 0))
        c = jnp.where((first | second)[None, :], cos[:, fidx], 1.0)
        s1 = jnp.where(first[None, :], -sin[:, fidx], 0.0)
        s2 = jnp.where(second[None, :], sin[:, fidx], 0.0)
        return c, s1, s2

    c64, s1_64, s2_64 = pattern(ROPE_THETA, PART_ROT, HEAD_DIM, 0, LANES)
    c32, s1_32, s2_32 = pattern(MLA_THETA, MLA_ROPE, MLA_ROPE, 0, LANES)
    ca, s1a, s2a = pattern(ROPE_THETA, PART_ROT, HEAD_DIM, M_IK, HEAD_DIM)
    cb, s1b, s2b = pattern(MLA_THETA, MLA_ROPE, MLA_ROPE, M_KR, MLA_ROPE)
    return jnp.stack([c64, s1_64, s2_64, c32, s1_32, s2_32, ca * cb, s1a, s2a, s1b, s2b], axis=0)


def _rot(x, c, s1, s2, half):
    return x * c + pltpu.roll(x, LANES - half, 1) * s1 + pltpu.roll(x, half, 1) * s2


def _post_kernel(dq_ref, iq_ref, ckv_ref, dk_ref, dv_ref, misc_ref, qpe_ref, tab_ref, bias_ref, gkv_ref,
                 dq_o, iq_o, ckv_o, dk_o, misc_o, qpe_o, pack_o):
    c64, s1, s2 = tab_ref[0], tab_ref[1], tab_ref[2]
    half = PART_ROT // 2
    for c in range(BRANCH_W // LANES):
        sl = slice(c * LANES, (c + 1) * LANES)
        dq_o[:, sl] = _rot(dq_ref[:, sl], c64, s1, s2, half)
        iq_o[:, sl] = _rot(iq_ref[:, sl], c64, s1, s2, half)
    dk = _rot(dk_ref[...], c64, s1, s2, half)
    dk_o[...] = dk

    c32, t1, t2 = tab_ref[3], tab_ref[4], tab_ref[5]
    for c in range(qpe_ref.shape[1] // LANES):
        sl = slice(c * LANES, (c + 1) * LANES)
        qpe_o[:, sl] = _rot(qpe_ref[:, sl], c32, t1, t2, MLA_ROPE // 2)

    ckv = ckv_ref[...]
    ckv_o[...] = ckv * lax.rsqrt(jnp.mean(ckv * ckv, axis=-1, keepdims=True) + EPS) * gkv_ref[...]

    x = misc_ref[...]
    roped = (x * tab_ref[6]
             + pltpu.roll(x, LANES - half, 1) * tab_ref[7] + pltpu.roll(x, half, 1) * tab_ref[8]
             + pltpu.roll(x, LANES - MLA_ROPE // 2, 1) * tab_ref[9] + pltpu.roll(x, MLA_ROPE // 2, 1) * tab_ref[10])
    lane = lax.broadcasted_iota(jnp.int32, x.shape, 1)
    logf = jax.nn.log_sigmoid(x + bias_ref[...])
    is_ff = jnp.where(lane >= M_FF, jnp.where(lane < M_IW, 1, 0), 0)
    misc = jnp.where(is_ff == 1, logf, roped)
    misc_o[...] = misc

    pack_o[:, 0:LANES] = dk.astype(BF16)
    pack_o[:, LANES:2 * LANES] = dv_ref[...].astype(BF16)
    pack_o[:, 2 * LANES:3 * LANES] = misc.astype(BF16)


def _post(z, qm, tabs, b_forget, g_kv, t_len):
    m = z.shape[0]
    tm = _row_tile(m)
    if t_len % tm == 0:
        nt = t_len // tm
        tab_map = lambda i: (0, i % nt, 0)
    else:
        assert tm % t_len == 0
        tabs = jnp.tile(tabs, (1, tm // t_len, 1))
        tab_map = lambda i: (0, 0, 0)
    bias = jnp.zeros((1, LANES), F32).at[0, M_FF:M_FF + HEADS].set(b_forget.astype(F32))
    blk = lambda w, cb: pl.BlockSpec((tm, w), lambda i: (i, cb))
    outs = [(BRANCH_W, F32), (BRANCH_W, F32), (LANES, F32), (LANES, F32), (LANES, F32), (HEADS * MLA_ROPE, F32),
            (3 * LANES, BF16)]
    return pl.pallas_call(
        _post_kernel,
        grid=(m // tm,),
        in_specs=[blk(BRANCH_W, C_DQ // BRANCH_W), blk(BRANCH_W, C_IQ // BRANCH_W), blk(LANES, C_CKV // LANES),
                  blk(LANES, C_DK // LANES), blk(LANES, C_DV // LANES), blk(LANES, C_MISC // LANES),
                  pl.BlockSpec((tm, HEADS * MLA_ROPE), lambda i: (i, BRANCH_W // (HEADS * MLA_ROPE))),
                  pl.BlockSpec((11, tm, LANES), tab_map),
                  pl.BlockSpec((1, LANES), lambda i: (0, 0)),
                  pl.BlockSpec((1, LANES), lambda i: (0, 0))],
        out_specs=[pl.BlockSpec((tm, w), lambda i: (i, 0)) for w, _ in outs],
        out_shape=[jax.ShapeDtypeStruct((m, w), dt) for w, dt in outs],
        compiler_params=_cparams("parallel"),
        name="post",
    )(z, z, z, z, z, z, qm, tabs, bias, g_kv.reshape(1, LANES).astype(F32))


def _cumsum_kernel(x_ref, o_ref):
    nc = x_ref.shape[1]
    row = lax.broadcasted_iota(jnp.int32, (LANES, LANES), 0)
    col = lax.broadcasted_iota(jnp.int32, (LANES, LANES), 1)
    tri = jnp.where(row <= col, 1.0, 0.0).astype(F32)

    def body(c, carry):
        cs = jnp.dot(x_ref[0, c], tri, precision=lax.Precision.HIGHEST, preferred_element_type=F32) + carry
        o_ref[0, c] = cs
        return cs[:, LANES - 1:LANES]

    lax.fori_loop(0, nc, body, jnp.zeros((HEADS, 1), F32))


def _cumsum(logf):
    b, lp, h = logf.shape
    nc = lp // LANES
    x = logf.transpose(0, 2, 1).reshape(b, h, nc, LANES).transpose(0, 2, 1, 3)
    out = pl.pallas_call(
        _cumsum_kernel,
        grid=(b,),
        in_specs=[pl.BlockSpec((1, nc, h, LANES), lambda i: (i, 0, 0, 0))],
        out_specs=pl.BlockSpec((1, nc, h, LANES), lambda i: (i, 0, 0, 0)),
        out_shape=jax.ShapeDtypeStruct((b, nc, h, LANES), F32),
        compiler_params=_cparams("parallel"),
        name="cumsum",
    )(x)
    return out.transpose(0, 2, 1, 3).reshape(b, h, lp)


def _last_allowed(q_last, per_frame):
    return q_last if per_frame else (q_last | (CHUNK - 1))


def _flash_kernel(*refs, tq, tk, off, kv_len, lp, scale, per_frame, has_bias, d2, k2_off):
    it = iter(refs)
    q_ref, k_ref, v_ref = next(it), next(it), next(it)
    q2_ref = k2_ref = qb_ref = kb_ref = None
    if d2:
        q2_ref, k2_ref = next(it), next(it)
    if has_bias:
        qb_ref, kb_ref = next(it), next(it)
    o_ref, m_scr, l_scr, acc_scr = next(it), next(it), next(it), next(it)

    i, j = pl.program_id(1), pl.program_id(2)
    q0 = i * tq + off
    k0 = j * tk

    @pl.when(j == 0)
    def _():
        m_scr[...] = jnp.full(m_scr.shape, NEG, F32)
        l_scr[...] = jnp.zeros(l_scr.shape, F32)
        acc_scr[...] = jnp.zeros(acc_scr.shape, F32)

    @pl.when(k0 <= _last_allowed(q0 + tq - 1, per_frame))
    def _():
        qpos = q0 + lax.broadcasted_iota(jnp.int32, (tq, tk), 0)
        kpos = k0 + lax.broadcasted_iota(jnp.int32, (tq, tk), 1)
        qlim = qpos if per_frame else (qpos | (CHUNK - 1))
        allowed = kpos <= qlim
        mask_bias = jnp.where(allowed, 0.0, NEG)
        if lp > kv_len:
            mask_bias = jnp.where(kpos < kv_len, mask_bias, NEG)
        for h in range(HEADS):
            hs = slice(h * HEAD_DIM, (h + 1) * HEAD_DIM)
            qh = (q_ref[0, :, hs].astype(F32) * scale).astype(BF16)
            kh = k_ref[0, :, hs].astype(BF16)
            s = lax.dot_general(qh, kh, (((1,), (1,)), ((), ())), preferred_element_type=F32)
            if d2:
                q2 = (q2_ref[0, :, h * d2:(h + 1) * d2].astype(F32) * scale).astype(BF16)
                k2 = k2_ref[0, :, k2_off:k2_off + d2].astype(BF16)
                s = s + lax.dot_general(q2, k2, (((1,), (1,)), ((), ())), preferred_element_type=F32)
            if has_bias:
                s = s + (qb_ref[0, :, h:h + 1] - kb_ref[0, h:h + 1, :])
            s = s + mask_bias
            m_prev = m_scr[h]
            m_new = jnp.maximum(m_prev, jnp.max(s, axis=1, keepdims=True))
            alpha = jnp.exp(m_prev - m_new)
            p = jnp.exp(s - m_new)
            l_scr[h] = alpha * l_scr[h] + jnp.sum(p, axis=1, keepdims=True)
            acc_scr[h] = alpha * acc_scr[h] + jnp.dot(p.astype(BF16), v_ref[0, :, hs].astype(BF16),
                                                      preferred_element_type=F32)
            m_scr[h] = m_new

    @pl.when(j == pl.num_programs(2) - 1)
    def _():
        for h in range(HEADS):
            o_ref[0, :, h * HEAD_DIM:(h + 1) * HEAD_DIM] = acc_scr[h] / l_scr[h]


def _flash(q, k, v, *, kv_len, per_frame, scale, q2=None, k2=None, bias=None):
    (qa, qcb), (ka, kcb), (va, vcb) = q, k, v
    b, t_len = qa.shape[0], qa.shape[1]
    lp = ka.shape[1]
    tq = min(512, t_len)
    tk = min(512, lp)
    off = kv_len - t_len
    nq, nk = t_len // tq, lp // tk

    def kidx(i, j):
        last = _last_allowed(i * tq + off + tq - 1, per_frame) // tk
        return jnp.minimum(j, jnp.minimum(last, nk - 1))

    w = HEADS * HEAD_DIM
    operands = [qa, ka, va]
    in_specs = [pl.BlockSpec((1, tq, w), lambda bb, i, j: (bb, i, qcb)),
                pl.BlockSpec((1, tk, w), lambda bb, i, j: (bb, kidx(i, j), kcb)),
                pl.BlockSpec((1, tk, w), lambda bb, i, j: (bb, kidx(i, j), vcb))]
    d2, k2_off = 0, 0
    if q2 is not None:
        (q2a, q2cb), (k2a, k2cb, k2_off, d2) = q2, k2
        k2w = k2a.shape[2] if k2a.shape[2] < LANES else LANES
        operands += [q2a, k2a]
        in_specs += [pl.BlockSpec((1, tq, HEADS * d2), lambda bb, i, j: (bb, i, q2cb)),
                     pl.BlockSpec((1, tk, k2w), lambda bb, i, j: (bb, kidx(i, j), k2cb))]
    if bias is not None:
        qb, kb = bias
        operands += [qb, kb]
        in_specs += [pl.BlockSpec((1, tq, HEADS), lambda bb, i, j: (bb, i, 0)),
                     pl.BlockSpec((1, HEADS, tk), lambda bb, i, j: (bb, 0, kidx(i, j)))]
    kern = functools.partial(_flash_kernel, tq=tq, tk=tk, off=off, kv_len=kv_len, lp=lp, scale=scale,
                             per_frame=per_frame, has_bias=bias is not None, d2=d2, k2_off=k2_off)
    return pl.pallas_call(
        kern,
        grid=(b, nq, nk),
        in_specs=in_specs,
        out_specs=pl.BlockSpec((1, tq, w), lambda bb, i, j: (bb, i, 0)),
        out_shape=jax.ShapeDtypeStruct((b, t_len, w), F32),
        scratch_shapes=[pltpu.VMEM((HEADS, tq, 1), F32), pltpu.VMEM((HEADS, tq, 1), F32),
                        pltpu.VMEM((HEADS, tq, HEAD_DIM), F32)],
        compiler_params=_cparams("parallel", "parallel", "arbitrary"),
        name="flash_fox" if per_frame else "flash_mla",
    )(*operands)


def _sortable(x):
    bits = pltpu.bitcast(x, jnp.int32)
    return jnp.where(bits < 0, bits ^ INT_MAX, bits)


def _dsa_kernel(q_ref, iq_ref, misc_ref, kv_ref, o_ref, key_scr, j_scr, m_scr, l_scr, acc_scr,
                *, tq, tk, off, kv_len, n_sel, idx_bits):
    i = pl.program_id(1)
    q0 = i * tq + off
    qlim = (q0 + lax.broadcasted_iota(jnp.int32, (tq, 1), 0)) | (CHUNK - 1)
    k_end = jnp.minimum(((q0 + tq - 1) | (CHUNK - 1)) + 1, kv_len)
    nkt = (k_end + tk - 1) // tk
    kv_rep = HEADS // DSA_KV_HEADS

    iw = misc_ref[0, :, M_IW:M_IW + HEADS].astype(F32)

    def score_tile(j, carry):
        ks = pl.multiple_of(j * tk, tk)
        ik = kv_ref[0, pl.ds(ks, tk), 2 * LANES + M_IK:2 * LANES + M_IK + HEAD_DIM]
        sc = jnp.zeros((tq, tk), F32)
        for h in range(HEADS):
            iqh = iq_ref[0, :, h * HEAD_DIM:(h + 1) * HEAD_DIM].astype(BF16)
            s = lax.dot_general(iqh, ik, (((1,), (1,)), ((), ())), preferred_element_type=F32)
            sc = sc + jnp.maximum(s, 0.0) * iw[:, h:h + 1]
        kpos = ks + lax.broadcasted_iota(jnp.int32, (tq, tk), 1)
        adm = jnp.where(kpos <= qlim, jnp.where(kpos < kv_len, 1, 0), 0)
        key_scr[j] = jnp.where(adm == 1, _sortable(sc), KEY_NEG_INF)
        return carry

    lax.fori_loop(0, nkt, score_tile, 0)

    def count(pred):
        def body(j, acc):
            ind = pred(key_scr[j], j)
            part = ind[:, 0:LANES]
            for c in range(1, tk // LANES):
                part = part + ind[:, c * LANES:(c + 1) * LANES]
            return acc + part

        acc = lax.fori_loop(0, nkt, body, jnp.zeros((tq, LANES), jnp.int32))
        return jnp.sum(acc, axis=1, keepdims=True)

    def count_ge(cand):
        return count(lambda kt, j: jnp.where(kt >= cand, 1, 0))

    def bit_step(s, ans):
        cand = ans + jnp.left_shift(jnp.int32(1), 31 - s)
        return jnp.where(count_ge(cand) >= n_sel, cand, ans)

    thr = lax.fori_loop(0, 32, bit_step, jnp.full((tq, 1), INT_MIN, jnp.int32))

    c_gt = count(lambda kt, j: jnp.where(kt > thr, 1, 0))
    c_eq = count(lambda kt, j: jnp.where(kt == thr, 1, 0))
    need = n_sel - c_gt
    j_scr[...] = jnp.full((tq, 1), INT_MAX, jnp.int32)

    @pl.when(jnp.max(c_eq - need) > 0)
    def _():
        def idx_step(s, jv):
            cand = jv + jnp.left_shift(jnp.int32(1), idx_bits - 1 - s)

            def pred(kt, j):
                idx = j * tk + lax.broadcasted_iota(jnp.int32, (tq, tk), 1)
                return jnp.where(kt == thr, jnp.where(idx < cand, 1, 0), 0)

            return jnp.where(count(pred) < need, cand, jv)

        j_scr[...] = lax.fori_loop(0, idx_bits, idx_step, jnp.zeros((tq, 1), jnp.int32))

    short = thr <= KEY_NEG_INF
    lo = jnp.where(short, KEY_NEG_INF, thr)
    j_lim = jnp.where(short, -1, j_scr[...])

    m_scr[...] = jnp.full(m_scr.shape, NEG, F32)
    l_scr[...] = jnp.zeros(l_scr.shape, F32)
    acc_scr[...] = jnp.zeros(acc_scr.shape, F32)

    def attend_tile(j, carry):
        ks = pl.multiple_of(j * tk, tk)
        kt = key_scr[j]
        idx = ks + lax.broadcasted_iota(jnp.int32, (tq, tk), 1)
        sel = jnp.where(kt > lo, 1, jnp.where(kt == lo, jnp.where(idx <= j_lim, 1, 0), 0))
        sel = jnp.where(kt < KEY_POS_INF, sel, 0)
        mask_bias = jnp.where(sel == 1, 0.0, NEG)
        for h in range(HEADS):
            g = h // kv_rep
            qh = (q_ref[0, :, h * HEAD_DIM:(h + 1) * HEAD_DIM].astype(F32) * (HEAD_DIM ** -0.5)).astype(BF16)
            kh = kv_ref[0, pl.ds(ks, tk), g * HEAD_DIM:(g + 1) * HEAD_DIM]
            vh = kv_ref[0, pl.ds(ks, tk), LANES + g * HEAD_DIM:LANES + (g + 1) * HEAD_DIM]
            s = lax.dot_general(qh, kh, (((1,), (1,)), ((), ())), preferred_element_type=F32) + mask_bias
            m_prev = m_scr[h]
            m_new = jnp.maximum(m_prev, jnp.max(s, axis=1, keepdims=True))
            alpha = jnp.exp(m_prev - m_new)
            p = jnp.exp(s - m_new)
            l_scr[h] = alpha * l_scr[h] + jnp.sum(p, axis=1, keepdims=True)
            acc_scr[h] = alpha * acc_scr[h] + jnp.dot(p.astype(BF16), vh, preferred_element_type=F32)
            m_scr[h] = m_new
        return carry

    lax.fori_loop(0, nkt, attend_tile, 0)
    for h in range(HEADS):
        o_ref[0, :, h * HEAD_DIM:(h + 1) * HEAD_DIM] = acc_scr[h] / l_scr[h]


def _dsa(dq, iq, misc, pack, *, kv_len):
    b, t_len, w = dq.shape
    lp = pack.shape[1]
    tq = min(128, t_len)
    tk = min(512, lp)
    n_sel = min(TOPK_MAX, kv_len // 4)
    assert tk >= n_sel and lp % tk == 0 and t_len % tq == 0
    nkt_max = lp // tk
    kern = functools.partial(_dsa_kernel, tq=tq, tk=tk, off=kv_len - t_len, kv_len=kv_len, n_sel=n_sel,
                             idx_bits=max(1, int(lp - 1).bit_length()))
    return pl.pallas_call(
        kern,
        grid=(b, t_len // tq),
        in_specs=[pl.BlockSpec((1, tq, w), lambda bb, i: (bb, i, 0)),
                  pl.BlockSpec((1, tq, w), lambda bb, i: (bb, i, 0)),
                  pl.BlockSpec((1, tq, LANES), lambda bb, i: (bb, i, 0)),
                  pl.BlockSpec((1, lp, 3 * LANES), lambda bb, i: (bb, 0, 0))],
        out_specs=pl.BlockSpec((1, tq, w), lambda bb, i: (bb, i, 0)),
        out_shape=jax.ShapeDtypeStruct((b, t_len, w), F32),
        scratch_shapes=[pltpu.VMEM((nkt_max, tq, tk), jnp.int32), pltpu.VMEM((tq, 1), jnp.int32),
                        pltpu.VMEM((HEADS, tq, 1), F32), pltpu.VMEM((HEADS, tq, 1), F32),
                        pltpu.VMEM((HEADS, tq, HEAD_DIM), F32)],
        compiler_params=_cparams("parallel", "arbitrary"),
        name="dsa",
    )(dq, iq, misc, pack)


def _merge_kernel(x_ref, oa_ref, ob_ref, oc_ref, gz_ref, wa_ref, wb_ref, wc_ref, wo_ref, o_ref):
    d = x_ref.shape[1]
    merged = jnp.zeros(x_ref.shape, F32)
    for n, (o_br, w_br) in enumerate(((oa_ref, wa_ref), (ob_ref, wb_ref), (oc_ref, wc_ref))):
        gate = jax.nn.sigmoid(gz_ref[:, n * d:(n + 1) * d])
        merged = merged + gate * jnp.dot(o_br[...].astype(BF16), w_br[...], preferred_element_type=F32)
    o_ref[...] = x_ref[...] + jnp.dot(merged.astype(BF16), wo_ref[...], preferred_element_type=F32)


def _merge(x, oa, ob, oc, z, wa, wb, wc, wo):
    m, d = x.shape
    tm = _row_tile(m)
    row = lambda w: pl.BlockSpec((tm, w), lambda i: (i, 0))
    full = lambda a: pl.BlockSpec(a.shape, lambda i: (0, 0))
    return pl.pallas_call(
        _merge_kernel,
        grid=(m // tm,),
        in_specs=[row(d), row(BRANCH_W), row(BRANCH_W), row(BRANCH_W), row(3 * d), full(wa), full(wb), full(wc), full(wo)],
        out_specs=row(d),
        out_shape=jax.ShapeDtypeStruct((m, d), F32),
        compiler_params=_cparams("parallel"),
        name="merge",
    )(x, oa, ob, oc, z, wa, wb, wc, wo)


def _ffn_kernel(x_ref, g_ref, wg_ref, wu_ref, wd_ref, o_ref, hn_scr, acc_scr):
    j = pl.program_id(1)

    @pl.when(j == 0)
    def _():
        x = x_ref[...]
        hn_scr[...] = (x * lax.rsqrt(jnp.mean(x * x, axis=-1, keepdims=True) + EPS) * g_ref[...]).astype(BF16)
        acc_scr[...] = jnp.zeros(acc_scr.shape, F32)

    hn = hn_scr[...]
    a = jnp.dot(hn, wg_ref[...], preferred_element_type=F32)
    u = jnp.dot(hn, wu_ref[...], preferred_element_type=F32)
    acc_scr[...] += jnp.dot((jax.nn.silu(a) * u).astype(BF16), wd_ref[...], preferred_element_type=F32)

    @pl.when(j == pl.num_programs(1) - 1)
    def _():
        o_ref[...] = x_ref[...] + acc_scr[...]


def _ffn(x, g, wg, wu, wd):
    m, d = x.shape
    dff = wg.shape[1]
    tm = _row_tile(m)
    tf = dff // 2 if (dff // 2) % LANES == 0 else dff
    return pl.pallas_call(
        _ffn_kernel,
        grid=(m // tm, dff // tf),
        in_specs=[pl.BlockSpec((tm, d), lambda i, j: (i, 0)),
                  pl.BlockSpec((1, d), lambda i, j: (0, 0)),
                  pl.BlockSpec((d, tf), lambda i, j: (0, j)),
                  pl.BlockSpec((d, tf), lambda i, j: (0, j)),
                  pl.BlockSpec((tf, d), lambda i, j: (j, 0))],
        out_specs=pl.BlockSpec((tm, d), lambda i, j: (i, 0)),
        out_shape=jax.ShapeDtypeStruct((m, d), F32),
        scratch_shapes=[pltpu.VMEM((tm, d), BF16), pltpu.VMEM((tm, d), F32)],
        compiler_params=_cparams("parallel", "arbitrary"),
        name="ffn",
    )(x, g.reshape(1, d).astype(F32), wg, wu, wd)


def _rmsnorm_kernel(x_ref, g_ref, o_ref):
    x = x_ref[...]
    o_ref[...] = x * lax.rsqrt(jnp.mean(x * x, axis=-1, keepdims=True) + EPS) * g_ref[...]


def _rmsnorm(x, g):
    m, d = x.shape
    tm = _row_tile(m)
    return pl.pallas_call(
        _rmsnorm_kernel,
        grid=(m // tm,),
        in_specs=[pl.BlockSpec((tm, d), lambda i: (i, 0)), pl.BlockSpec((1, d), lambda i: (0, 0))],
        out_specs=pl.BlockSpec((tm, d), lambda i: (i, 0)),
        out_shape=jax.ShapeDtypeStruct((m, d), F32),
        compiler_params=_cparams("parallel"),
        name="final_rmsnorm",
    )(x, g.reshape(1, d).astype(F32))


def _pack_weights(w_in, w_mla_uq, w_mla_ukv):
    sizes = (BRANCH_W, BRANCH_W, BRANCH_W, HEADS, 256, 128, MLA_ROPE, BRANCH_W, 128, 128, BRANCH_W, HEAD_DIM, HEADS)
    offs = np.concatenate([[0], np.cumsum(sizes)])
    (fq, fk, fv, ff, cq, ckv, kr, dq, dk, dv, iq, ik, iw) = [w_in[..., int(offs[n]):int(offs[n + 1])]
                                                            for n in range(len(sizes))]
    gz = w_in[..., int(offs[-1]):]
    pad = jnp.zeros(w_in.shape[:-1] + (LANES - (HEAD_DIM + MLA_ROPE + 2 * HEADS),), w_in.dtype)
    w_in_p = jnp.concatenate([gz, fq, fk, fv, dq, iq, cq, ckv, dk, dv, ik, kr, ff, iw, pad], axis=-1).astype(BF16)
    assert w_in_p.shape[-1] == N_PACK

    depth = w_in.shape[0]
    uq = w_mla_uq.reshape(depth, w_mla_uq.shape[1], HEADS, HEAD_DIM + MLA_ROPE)
    w_uq_p = jnp.concatenate([uq[..., :HEAD_DIM].reshape(depth, -1, HEADS * HEAD_DIM),
                              uq[..., HEAD_DIM:].reshape(depth, -1, HEADS * MLA_ROPE)], axis=-1).astype(BF16)
    ukv = w_mla_ukv.reshape(depth, w_mla_ukv.shape[1], HEADS, 2 * HEAD_DIM)
    w_ukv_p = jnp.concatenate([ukv[..., :HEAD_DIM].reshape(depth, -1, HEADS * HEAD_DIM),
                               ukv[..., HEAD_DIM:].reshape(depth, -1, HEADS * HEAD_DIM)], axis=-1).astype(BF16)
    return w_in_p, w_uq_p, w_ukv_p


def _with_past(past, new, lp, dtype=None):
    parts = [new] if past is None else [past.astype(new.dtype), new]
    rows = sum(p.shape[1] for p in parts)
    if lp > rows:
        parts.append(jnp.zeros((new.shape[0], lp - rows, new.shape[2]), new.dtype))
    out = parts[0] if len(parts) == 1 else jnp.concatenate(parts, axis=1)
    return out if dtype is None else out.astype(dtype)


def _layer(x, past, tabs, p):
    b, t_len, d = x.shape
    m = b * t_len
    p_len = 0 if past is None else past[0].shape[1]
    kv_len = p_len + t_len
    lp = _round_up(kv_len, min(512, _round_up(kv_len, LANES)))
    x2 = x.reshape(m, d)

    z = _rms_matmul(x2, 0, d, p["norm_attn"], p["w_in"], tn=N_PACK // 5)
    qm = _rms_matmul(z, C_CQ // 256, 256, p["norm_mla_q"], p["w_uq"], tn=256)
    dq_r, iq_r, ckv_n, dk_r, misc, qpe_r, pack = _post(z, qm, tabs, p["b_forget"], p["norm_mla_kv"], t_len)

    z3 = z.reshape(b, t_len, N_PACK)
    r3 = lambda a: a.reshape(b, t_len, a.shape[-1])
    fk = z3[..., C_FK:C_FK + BRANCH_W]
    fv = z3[..., C_FV:C_FV + BRANCH_W]
    dv = z3[..., C_DV:C_DV + LANES]
    misc3 = r3(misc)
    logf = misc3[..., M_FF:M_FF + HEADS]
    kpe = misc3[..., M_KR:M_KR + MLA_ROPE]
    ik = misc3[..., M_IK:M_IK + HEAD_DIM]
    new_state = (fk.reshape(b, t_len, HEADS, HEAD_DIM), fv.reshape(b, t_len, HEADS, HEAD_DIM), logf,
                 r3(ckv_n), kpe, r3(dk_r).reshape(b, t_len, DSA_KV_HEADS, HEAD_DIM),
                 dv.reshape(b, t_len, DSA_KV_HEADS, HEAD_DIM), ik)

    if past is None:
        fox_q, fox_k, fox_v = (z3, C_FQ // BRANCH_W), (z3, C_FK // BRANCH_W), (z3, C_FV // BRANCH_W)
        logf_all = _with_past(None, logf, lp)
        ckv_all = r3(ckv_n)
        kpe_src = (misc3, 0, M_KR, MLA_ROPE)
        pack_all = r3(pack)
    else:
        (c_fk, c_fv, c_logf, c_ckv, c_kpe, c_dk, c_dv, c_ik) = past
        flat = lambda a: a.reshape(b, p_len, -1)
        fox_q = (z3, C_FQ // BRANCH_W)
        fox_k = (_with_past(flat(c_fk), fk, lp), 0)
        fox_v = (_with_past(flat(c_fv), fv, lp), 0)
        logf_all = _with_past(c_logf, logf, lp)
        ckv_all = _with_past(c_ckv, r3(ckv_n), lp)
        kpe_src = (_with_past(c_kpe, kpe, lp), 0, 0, MLA_ROPE)
        past_pack = jnp.concatenate([flat(c_dk), flat(c_dv), c_ik,
                                     jnp.zeros((b, p_len, LANES - HEAD_DIM), F32)], axis=-1).astype(BF16)
        pack_all = _with_past(past_pack, r3(pack), lp)
    if ckv_all.shape[1] < lp:
        ckv_all = _with_past(None, ckv_all, lp)

    cum = _cumsum(logf_all)
    q_bias = cum[:, :, kv_len - t_len:kv_len].transpose(0, 2, 1)
    o_a = _flash(fox_q, fox_k, fox_v, kv_len=kv_len, per_frame=True, scale=HEAD_DIM ** -0.5, bias=(q_bias, cum))

    kv = _rms_matmul(ckv_all.reshape(b * lp, LANES), 0, LANES, None, p["w_ukv"], tn=512, out_dtype=BF16)
    kv3 = kv.reshape(b, lp, 2 * BRANCH_W)
    qm3 = qm.reshape(b, t_len, -1)
    o_b = _flash((qm3, 0), (kv3, 0), (kv3, 1), kv_len=kv_len, per_frame=False,
                 scale=(HEAD_DIM + MLA_ROPE) ** -0.5, q2=(r3(qpe_r), 0), k2=kpe_src)

    o_c = _dsa(r3(dq_r), r3(iq_r), misc3, pack_all, kv_len=kv_len)

    x2 = _merge(x2, o_a.reshape(m, -1), o_b.reshape(m, -1), o_c.reshape(m, -1), z,
                p["w_br_fox"], p["w_br_mla"], p["w_br_dsa"], p["w_out"])
    x2 = _ffn(x2, p["norm_ffn"], p["w_ffn_gate"], p["w_ffn_up"], p["w_ffn_down"])
    return x2.reshape(b, t_len, d), new_state


def kernel(x_prompt, x_sample, cache_fox_k, cache_fox_v, cache_fox_logf, cache_mla_ckv, cache_mla_kpe, cache_dsa_k, cache_dsa_v, cache_dsa_idxk, norm_attn, w_in, b_forget, norm_mla_q, norm_mla_kv, w_mla_uq, w_mla_ukv, w_br_fox, w_br_mla, w_br_dsa, w_out, norm_ffn, w_ffn_gate, w_ffn_up, w_ffn_down, norm_final):
    depth = w_in.shape[0]
    past_len = cache_fox_k.shape[2]
    w_in_p, w_uq_p, w_ukv_p = _pack_weights(w_in, w_mla_uq, w_mla_ukv)
    bf = lambda a: a.astype(BF16)
    w_br_fox, w_br_mla, w_br_dsa, w_out = bf(w_br_fox), bf(w_br_mla), bf(w_br_dsa), bf(w_out)
    w_ffn_gate, w_ffn_up, w_ffn_down = bf(w_ffn_gate), bf(w_ffn_up), bf(w_ffn_down)
    caches = (cache_fox_k, cache_fox_v, cache_fox_logf, cache_mla_ckv, cache_mla_kpe,
              cache_dsa_k, cache_dsa_v, cache_dsa_idxk)

    tabs_p = _rope_tables(jnp.arange(x_prompt.shape[1]))
    tabs_s = _rope_tables(past_len + jnp.arange(x_sample.shape[1]))

    xp, xs = x_prompt, x_sample
    states_p, states_s = [], []
    for l in range(depth):
        p = dict(norm_attn=norm_attn[l], w_in=w_in_p[l], b_forget=b_forget[l], norm_mla_q=norm_mla_q[l],
                 norm_mla_kv=norm_mla_kv[l], w_uq=w_uq_p[l], w_ukv=w_ukv_p[l], w_br_fox=w_br_fox[l],
                 w_br_mla=w_br_mla[l], w_br_dsa=w_br_dsa[l], w_out=w_out[l], norm_ffn=norm_ffn[l],
                 w_ffn_gate=w_ffn_gate[l], w_ffn_up=w_ffn_up[l], w_ffn_down=w_ffn_down[l])
        xp, st_p = _layer(xp, None, tabs_p, p)
        xs, st_s = _layer(xs, tuple(c[l] for c in caches), tabs_s, p)
        states_p.append(st_p)
        states_s.append(st_s)

    stacked_p = [jnp.stack(a, axis=0) for a in zip(*states_p)]
    stacked_s = [jnp.stack(a, axis=0) for a in zip(*states_s)]
    yp = _rmsnorm(xp.reshape(-1, xp.shape[-1]), norm_final).reshape(xp.shape)
    ys = _rmsnorm(xs.reshape(-1, xs.shape[-1]), norm_final).reshape(xs.shape)
    out = [yp, ys]
    for sp, ss in zip(stacked_p, stacked_s):
        out += [sp, ss]
    return tuple(out)
```

```python
import functools
import math

import numpy as np
import jax
import jax.numpy as jnp
from jax import lax
from jax.experimental import pallas as pl
from jax.experimental.pallas import tpu as pltpu

F32 = jnp.float32
BF16 = jnp.bfloat16

CHUNK = 64
EPS = 1e-6
HEADS = 8
HEAD_DIM = 64
MLA_ROPE = 32
MLA_THETA = 10000.0
DSA_KV_HEADS = 2
ROPE_THETA = 500000.0
PART_ROT = HEAD_DIM // 4
TOPK_MAX = 256
BRANCH_W = HEADS * HEAD_DIM

LANES = 128
SUBLANES = 8
VMEM_LIMIT = 56 * 1024 * 1024
NEG = -1e30
LOG2E = math.log2(math.e)
HS = LANES
QW = HEADS * HS

C_GZ = 0
C_FQ = 3072
C_FK = 4096
C_DQ = 5120
C_IQ = 6144
C_FV = 7168
C_CQ = 7680
C_CKV = 7936
C_DK = 8064
C_DV = 8192
C_MISC = 8320
N_PACK = 8448
M_IK, M_KR, M_FF, M_IW = 0, 64, 96, 104

KEY_NEG_INF = int(np.array(0xFF800000 ^ 0x7FFFFFFF, dtype=np.uint32).view(np.int32))
KEY_POS_INF = 0x7F800000
INT_MIN = -(2 ** 31)
INT_MAX = 2 ** 31 - 1

NT_DIMS = (((1,), (1,)), ((), ()))


def _cparams(*sem):
    return pltpu.CompilerParams(dimension_semantics=sem, vmem_limit_bytes=VMEM_LIMIT)


def _round_up(a, b):
    return -(-a // b) * b


def _row_tile(m, cap=512):
    t = cap
    while t > 8 and m % t:
        t //= 2
    assert m % t == 0
    return t


def _rms_mm_kernel(x_ref, g_ref, w_ref, o_ref, xn_ref, *, normalize):
    @pl.when(pl.program_id(1) == 0)
    def _():
        x = x_ref[...].astype(F32)
        if normalize:
            x = x * lax.rsqrt(jnp.mean(x * x, axis=-1, keepdims=True) + EPS) * g_ref[...]
        xn_ref[...] = x.astype(BF16)

    o_ref[...] = jnp.dot(xn_ref[...], w_ref[...], preferred_element_type=F32).astype(o_ref.dtype)


def _rms_matmul(x, col_block, k, g, w, *, tn, out_dtype=F32):
    m = x.shape[0]
    n = w.shape[1]
    tm = _row_tile(m)
    normalize = g is not None
    if g is None:
        g = jnp.ones((k,), F32)
    return pl.pallas_call(
        functools.partial(_rms_mm_kernel, normalize=normalize),
        grid=(m // tm, n // tn),
        in_specs=[pl.BlockSpec((tm, k), lambda i, j: (i, col_block)),
                  pl.BlockSpec((1, k), lambda i, j: (0, 0)),
                  pl.BlockSpec((k, tn), lambda i, j: (0, j))],
        out_specs=pl.BlockSpec((tm, tn), lambda i, j: (i, j)),
        out_shape=jax.ShapeDtypeStruct((m, n), out_dtype),
        scratch_shapes=[pltpu.VMEM((tm, k), BF16)],
        compiler_params=_cparams("parallel", "arbitrary"),
        name="rms_matmul",
    )(x, g.reshape(1, k).astype(F32), w)


N_TABS = 12
TAB_IQ, TAB_DK, TAB_DQ_HI, TAB_MLA = 0, 3, 6, 9


def _rope_tables(pos):
    lane = np.arange(LANES)

    def pattern(theta, rot, period, lane0, width):
        half = rot // 2
        inv_freq = theta ** (-jnp.arange(half, dtype=F32) * (2.0 / rot))
        ang = pos.astype(F32)[:, None] * inv_freq[None, :]
        cos, sin = jnp.cos(ang), jnp.sin(ang)
        inside = (lane >= lane0) & (lane < lane0 + width)
        r = (lane - lane0) % period
        first = inside & (r < half)
        second = inside & (r >= half) & (r < rot)
        fidx = np.where(first, r, np.where(second, r - half, 0))
        c = jnp.where((first | second)[None, :], cos[:, fidx], 1.0)
        s1 = jnp.where(first[None, :], -sin[:, fidx], 0.0)
        s2 = jnp.where(second[None, :], sin[:, fidx], 0.0)
        return [c, s1, s2]

    tabs = (pattern(ROPE_THETA, PART_ROT, HS, 0, HS)
            + pattern(ROPE_THETA, PART_ROT, HEAD_DIM, 0, LANES)
            + pattern(ROPE_THETA, PART_ROT, HEAD_DIM, HEAD_DIM, HEAD_DIM)
            + pattern(MLA_THETA, MLA_ROPE, MLA_ROPE, M_KR, MLA_ROPE))
    return jnp.stack(tabs, axis=0)


def _rot(x, tab_ref, t0, half):
    return (x * tab_ref[t0] + pltpu.roll(x, LANES - half, 1) * tab_ref[t0 + 1]
            + pltpu.roll(x, half, 1) * tab_ref[t0 + 2])


def _post_kernel(fq_ref, dq_ref, iq_ref, ckv_ref, dk_ref, misc_ref, qm_ref, tab_ref, bias_ref, gkv_ref,
                 fqa_o, dqa_o, iqa_o, mqa_o, ckv_o, dk_o, misc_o):
    half = PART_ROT // 2
    kv_rep = HEADS // DSA_KV_HEADS
    sc_dot = (HEAD_DIM ** -0.5) * LOG2E
    sc_mla = ((HEAD_DIM + MLA_ROPE) ** -0.5) * LOG2E
    for h in range(HEADS):
        sl = slice(h * HS, (h + 1) * HS)
        fqa_o[:, sl] = (fq_ref[:, sl] * sc_dot).astype(BF16)
        t_dq = TAB_IQ if h < kv_rep else TAB_DQ_HI
        dqa_o[:, sl] = (_rot(dq_ref[:, sl], tab_ref, t_dq, half) * sc_dot).astype(BF16)
        iqa_o[:, sl] = _rot(iq_ref[:, sl], tab_ref, TAB_IQ, half).astype(BF16)
        mqa_o[:, sl] = (_rot(qm_ref[:, sl], tab_ref, TAB_MLA, MLA_ROPE // 2) * sc_mla).astype(BF16)
    dk_o[...] = _rot(dk_ref[...], tab_ref, TAB_DK, half)

    ckv = ckv_ref[...]
    ckv_o[...] = ckv * lax.rsqrt(jnp.mean(ckv * ckv, axis=-1, keepdims=True) + EPS) * gkv_ref[...]

    x = misc_ref[...]
    roped = (x * (tab_ref[TAB_IQ] * tab_ref[TAB_MLA])
             + pltpu.roll(x, LANES - half, 1) * tab_ref[TAB_IQ + 1] + pltpu.roll(x, half, 1) * tab_ref[TAB_IQ + 2]
             + pltpu.roll(x, LANES - MLA_ROPE // 2, 1) * tab_ref[TAB_MLA + 1]
             + pltpu.roll(x, MLA_ROPE // 2, 1) * tab_ref[TAB_MLA + 2])
    lane = lax.broadcasted_iota(jnp.int32, x.shape, 1)
    logf = jax.nn.log_sigmoid(x + bias_ref[...])
    is_ff = jnp.where(lane >= M_FF, jnp.where(lane < M_IW, 1, 0), 0)
    misc_o[...] = jnp.where(is_ff == 1, logf, roped)


def _post(z, qm, tabs, b_forget, g_kv, t_len):
    m = z.shape[0]
    tm = _row_tile(m)
    if t_len % tm == 0:
        nt = t_len // tm
        tab_map = lambda i: (0, i % nt, 0)
    else:
        assert tm % t_len == 0
        tabs = jnp.tile(tabs, (1, tm // t_len, 1))
        tab_map = lambda i: (0, 0, 0)
    bias = jnp.zeros((1, LANES), F32).at[0, M_FF:M_FF + HEADS].set(b_forget.astype(F32))
    blk = lambda w, c0: pl.BlockSpec((tm, w), lambda i: (i, c0 // w))
    outs = [(QW, BF16), (QW, BF16), (QW, BF16), (QW, BF16), (LANES, F32), (LANES, F32), (LANES, F32)]
    return pl.pallas_call(
        _post_kernel,
        grid=(m // tm,),
        in_specs=[blk(QW, C_FQ), blk(QW, C_DQ), blk(QW, C_IQ), blk(LANES, C_CKV), blk(LANES, C_DK),
                  blk(LANES, C_MISC), pl.BlockSpec((tm, QW), lambda i: (i, 0)),
                  pl.BlockSpec((N_TABS, tm, LANES), tab_map),
                  pl.BlockSpec((1, LANES), lambda i: (0, 0)),
                  pl.BlockSpec((1, LANES), lambda i: (0, 0))],
        out_specs=[pl.BlockSpec((tm, w), lambda i: (i, 0)) for w, _ in outs],
        out_shape=[jax.ShapeDtypeStruct((m, w), dt) for w, dt in outs],
        compiler_params=_cparams("parallel"),
        name="post",
    )(z, z, z, z, z, z, qm, tabs, bias, g_kv.reshape(1, LANES).astype(F32))


def _cumsum_kernel(x_ref, o_ref):
    nc = x_ref.shape[1]
    row = lax.broadcasted_iota(jnp.int32, (LANES, LANES), 0)
    col = lax.broadcasted_iota(jnp.int32, (LANES, LANES), 1)
    tri = jnp.where(row <= col, 1.0, 0.0).astype(F32)

    def body(c, carry):
        cs = jnp.dot(x_ref[0, c], tri, precision=lax.Precision.HIGHEST, preferred_element_type=F32) + carry
        o_ref[0, c] = cs
        return cs[:, LANES - 1:LANES]

    lax.fori_loop(0, nc, body, jnp.zeros((HEADS, 1), F32))


def _cumsum(logf):
    b, lp, h = logf.shape
    nc = lp // LANES
    x = logf.transpose(0, 2, 1).reshape(b, h, nc, LANES).transpose(0, 2, 1, 3)
    out = pl.pallas_call(
        _cumsum_kernel,
        grid=(b,),
        in_specs=[pl.BlockSpec((1, nc, h, LANES), lambda i: (i, 0, 0, 0))],
        out_specs=pl.BlockSpec((1, nc, h, LANES), lambda i: (i, 0, 0, 0)),
        out_shape=jax.ShapeDtypeStruct((b, nc, h, LANES), F32),
        compiler_params=_cparams("parallel"),
        name="cumsum",
    )(x)
    return out.transpose(0, 2, 1, 3).reshape(b, h, lp)


def _kside_kernel(k_ref, v_ref, ka_o, vt_o):
    ka_o[0] = k_ref[0].astype(BF16)
    vt_o[0, 0] = v_ref[0].T.astype(BF16)


def _kside(k, v, tk):
    (ka, kcb, kw), (va, vcb, vw) = k, v
    b, lp = ka.shape[0], ka.shape[1]
    nkt = lp // tk
    return pl.pallas_call(
        _kside_kernel,
        grid=(b, nkt),
        in_specs=[pl.BlockSpec((1, tk, kw), lambda bb, j: (bb, j, kcb)),
                  pl.BlockSpec((1, tk, vw), lambda bb, j: (bb, j, vcb))],
        out_specs=[pl.BlockSpec((1, tk, kw), lambda bb, j: (bb, j, 0)),
                   pl.BlockSpec((1, 1, vw, tk), lambda bb, j: (bb, j, 0, 0))],
        out_shape=[jax.ShapeDtypeStruct((b, lp, kw), BF16), jax.ShapeDtypeStruct((b, nkt, vw, tk), BF16)],
        compiler_params=_cparams("parallel", "parallel"),
        name="kside",
    )(ka, va)


def _mla_kv_kernel(ckv_ref, kpe_ref, wk_ref, wv_ref, ka_o, vt_o):
    c = ckv_ref[0].astype(BF16)
    k = jnp.dot(c, wk_ref[...], preferred_element_type=F32)
    kpe = kpe_ref[0]
    lane = lax.broadcasted_iota(jnp.int32, kpe.shape, 1)
    kpe = jnp.where(lane >= M_KR, jnp.where(lane < M_KR + MLA_ROPE, kpe, 0.0), 0.0)
    for h in range(HEADS):
        sl = slice(h * HS, (h + 1) * HS)
        ka_o[0, :, sl] = (k[:, sl] + kpe).astype(BF16)
    v = jnp.dot(c, wv_ref[...], preferred_element_type=F32)
    vt_o[0, 0] = v.T.astype(BF16)


def _mla_kv(ckv_all, kpe_all, wk, wv, tk):
    b, lp, _ = ckv_all.shape
    nkt = lp // tk
    return pl.pallas_call(
        _mla_kv_kernel,
        grid=(b, nkt),
        in_specs=[pl.BlockSpec((1, tk, LANES), lambda bb, j: (bb, j, 0)),
                  pl.BlockSpec((1, tk, LANES), lambda bb, j: (bb, j, 0)),
                  pl.BlockSpec(wk.shape, lambda bb, j: (0, 0)),
                  pl.BlockSpec(wv.shape, lambda bb, j: (0, 0))],
        out_specs=[pl.BlockSpec((1, tk, QW), lambda bb, j: (bb, j, 0)),
                   pl.BlockSpec((1, 1, BRANCH_W, tk), lambda bb, j: (bb, j, 0, 0))],
        out_shape=[jax.ShapeDtypeStruct((b, lp, QW), BF16), jax.ShapeDtypeStruct((b, nkt, BRANCH_W, tk), BF16)],
        compiler_params=_cparams("parallel", "parallel"),
        name="mla_kv",
    )(ckv_all, kpe_all, wk, wv)


def _softmax_step(h, s_t, vt_h, m_scr, l_scr, acc_scr):
    m_prev = m_scr[h]
    m_new = jnp.maximum(m_prev, jnp.max(s_t, axis=0, keepdims=True))
    alpha = jnp.exp2(m_prev - m_new)
    p_t = jnp.exp2(s_t - m_new)
    l_scr[h] = alpha * l_scr[h] + jnp.sum(p_t, axis=0, keepdims=True)
    rows = slice(h * HEAD_DIM, (h + 1) * HEAD_DIM)
    acc_scr[rows, :] = alpha * acc_scr[rows, :] + jnp.dot(vt_h, p_t.astype(BF16), preferred_element_type=F32)
    m_scr[h] = m_new


def _softmax_init(m_scr, l_scr, acc_scr):
    m_scr[...] = jnp.full(m_scr.shape, NEG, F32)
    l_scr[...] = jnp.zeros(l_scr.shape, F32)
    acc_scr[...] = jnp.zeros(acc_scr.shape, F32)


def _softmax_finish(o_ref, l_scr, acc_scr):
    for h in range(HEADS):
        rows = slice(h * HEAD_DIM, (h + 1) * HEAD_DIM)
        acc_scr[rows, :] = acc_scr[rows, :] / l_scr[h]
    o_ref[0] = acc_scr[...].T


def _last_allowed(q_pos, per_frame):
    return q_pos if per_frame else (q_pos | (CHUNK - 1))


def _flash_kernel(*refs, tq, tk, off, kv_len, per_frame, has_bias):
    if has_bias:
        qa_ref, ka_ref, vt_ref, qb_ref, kb_ref, o_ref, m_scr, l_scr, acc_scr = refs
    else:
        qa_ref, ka_ref, vt_ref, o_ref, m_scr, l_scr, acc_scr = refs
    i, j = pl.program_id(1), pl.program_id(2)
    q0 = i * tq + off
    k0 = j * tk
    k_last = k0 + tk - 1

    @pl.when(j == 0)
    def _():
        _softmax_init(m_scr, l_scr, acc_scr)

    def step(masked):
        if masked:
            kpos = k0 + lax.broadcasted_iota(jnp.int32, (tk, tq), 0)
            qlim = _last_allowed(q0 + lax.broadcasted_iota(jnp.int32, (tk, tq), 1), per_frame)
            mask_bias = jnp.where(kpos <= qlim, jnp.where(kpos < kv_len, 0.0, NEG), NEG)
        for h in range(HEADS):
            sl = slice(h * HS, (h + 1) * HS)
            s_t = lax.dot_general(ka_ref[0, :, sl], qa_ref[0, :, sl], NT_DIMS, preferred_element_type=F32)
            if has_bias:
                s_t = s_t + (qb_ref[0, h:h + 1, :] - kb_ref[0, :, h:h + 1])
            if masked:
                s_t = s_t + mask_bias
            _softmax_step(h, s_t, vt_ref[0, 0, h * HEAD_DIM:(h + 1) * HEAD_DIM, :], m_scr, l_scr, acc_scr)

    clear = jnp.logical_and(k_last <= _last_allowed(q0, per_frame), k_last < kv_len)
    needed = k0 <= _last_allowed(q0 + tq - 1, per_frame)

    @pl.when(clear)
    def _():
        step(False)

    @pl.when(jnp.logical_and(needed, jnp.logical_not(clear)))
    def _():
        step(True)

    @pl.when(j == pl.num_programs(2) - 1)
    def _():
        _softmax_finish(o_ref, l_scr, acc_scr)


def _flash(qa, ka, vt, *, t_len, kv_len, per_frame, bias=None):
    b, tp, _ = qa.shape
    lp = ka.shape[1]
    tk = vt.shape[3]
    tq = min(512, tp)
    off = kv_len - t_len
    nq, nk = tp // tq, lp // tk

    def kidx(i, j):
        last = _last_allowed(i * tq + off + tq - 1, per_frame) // tk
        return jnp.minimum(j, jnp.minimum(last, nk - 1))

    operands = [qa, ka, vt]
    in_specs = [pl.BlockSpec((1, tq, QW), lambda bb, i, j: (bb, i, 0)),
                pl.BlockSpec((1, tk, QW), lambda bb, i, j: (bb, kidx(i, j), 0)),
                pl.BlockSpec((1, 1, BRANCH_W, tk), lambda bb, i, j: (bb, kidx(i, j), 0, 0))]
    if bias is not None:
        operands += list(bias)
        in_specs += [pl.BlockSpec((1, HEADS, tq), lambda bb, i, j: (bb, 0, i)),
                     pl.BlockSpec((1, tk, HEADS), lambda bb, i, j: (bb, kidx(i, j), 0))]
    kern = functools.partial(_flash_kernel, tq=tq, tk=tk, off=off, kv_len=kv_len, per_frame=per_frame,
                             has_bias=bias is not None)
    return pl.pallas_call(
        kern,
        grid=(b, nq, nk),
        in_specs=in_specs,
        out_specs=pl.BlockSpec((1, tq, BRANCH_W), lambda bb, i, j: (bb, i, 0)),
        out_shape=jax.ShapeDtypeStruct((b, tp, BRANCH_W), F32),
        scratch_shapes=[pltpu.VMEM((HEADS, 1, tq), F32), pltpu.VMEM((HEADS, 1, tq), F32),
                        pltpu.VMEM((BRANCH_W, tq), F32)],
        compiler_params=_cparams("parallel", "parallel", "arbitrary"),
        name="flash_fox" if per_frame else "flash_mla",
    )(*operands)


def _sortable(x):
    bits = pltpu.bitcast(x, jnp.int32)
    return jnp.where(bits < 0, bits ^ INT_MAX, bits)


def _dsa_kernel(dqa_ref, iqa_ref, iw_ref, kk_ref, vt_ref, ik_ref, o_ref, key_scr, j_scr, m_scr, l_scr, acc_scr,
                *, tq, tk, off, kv_len, n_sel, idx_bits):
    i = pl.program_id(1)
    q0 = i * tq + off
    qlim = (q0 + lax.broadcasted_iota(jnp.int32, (1, tq), 1)) | (CHUNK - 1)
    k_end = jnp.minimum(((q0 + tq - 1) | (CHUNK - 1)) + 1, kv_len)
    nkt = (k_end + tk - 1) // tk
    kv_rep = HEADS // DSA_KV_HEADS

    def score_tile(j, carry):
        ks = pl.multiple_of(j * tk, tk)
        ik = ik_ref[0, pl.ds(ks, tk), :]
        sc = jnp.zeros((tk, tq), F32)
        for h in range(HEADS):
            s = lax.dot_general(ik, iqa_ref[0, :, h * HS:(h + 1) * HS], NT_DIMS, preferred_element_type=F32)
            sc = sc + jnp.maximum(s, 0.0) * iw_ref[0, h:h + 1, :]
        kpos = ks + lax.broadcasted_iota(jnp.int32, (tk, tq), 0)
        adm = jnp.where(kpos <= qlim, jnp.where(kpos < kv_len, 1, 0), 0)
        key_scr[j] = jnp.where(adm == 1, _sortable(sc), KEY_NEG_INF)
        return carry

    lax.fori_loop(0, nkt, score_tile, 0)

    def count(pred):
        def body(j, acc):
            ind = pred(key_scr[j], j)
            return acc + jnp.sum(ind.reshape(tk // SUBLANES, SUBLANES, tq), axis=0)

        acc = lax.fori_loop(0, nkt, body, jnp.zeros((SUBLANES, tq), jnp.int32))
        return jnp.sum(acc, axis=0, keepdims=True)

    def count_ge(cand):
        return count(lambda kt, j: jnp.where(kt >= cand, 1, 0))

    def bit_step(s, ans):
        cand = ans + jnp.left_shift(jnp.int32(1), 31 - s)
        return jnp.where(count_ge(cand) >= n_sel, cand, ans)

    thr = lax.fori_loop(0, 32, bit_step, jnp.full((1, tq), INT_MIN, jnp.int32))

    c_gt = count(lambda kt, j: jnp.where(kt > thr, 1, 0))
    c_eq = count(lambda kt, j: jnp.where(kt == thr, 1, 0))
    need = n_sel - c_gt
    j_scr[...] = jnp.full((1, tq), INT_MAX, jnp.int32)

    @pl.when(jnp.max(c_eq - need) > 0)
    def _():
        def idx_step(s, jv):
            cand = jv + jnp.left_shift(jnp.int32(1), idx_bits - 1 - s)

            def pred(kt, j):
                idx = j * tk + lax.broadcasted_iota(jnp.int32, (tk, tq), 0)
                return jnp.where(kt == thr, jnp.where(idx < cand, 1, 0), 0)

            return jnp.where(count(pred) < need, cand, jv)

        j_scr[...] = lax.fori_loop(0, idx_bits, idx_step, jnp.zeros((1, tq), jnp.int32))

    short = thr <= KEY_NEG_INF
    lo = jnp.where(short, KEY_NEG_INF, thr)
    j_lim = jnp.where(short, -1, j_scr[...])

    _softmax_init(m_scr, l_scr, acc_scr)

    def attend_tile(j, carry):
        ks = pl.multiple_of(j * tk, tk)
        kt = key_scr[j]
        idx = ks + lax.broadcasted_iota(jnp.int32, (tk, tq), 0)
        sel = jnp.where(kt > lo, 1, jnp.where(kt == lo, jnp.where(idx <= j_lim, 1, 0), 0))
        sel = jnp.where(kt < KEY_POS_INF, sel, 0)
        mask_bias = jnp.where(sel == 1, 0.0, NEG)
        kk = kk_ref[0, pl.ds(ks, tk), :]
        for h in range(HEADS):
            g = h // kv_rep
            s_t = lax.dot_general(kk, dqa_ref[0, :, h * HS:(h + 1) * HS], NT_DIMS, preferred_element_type=F32)
            _softmax_step(h, s_t + mask_bias, vt_ref[0, j, g * HEAD_DIM:(g + 1) * HEAD_DIM, :], m_scr, l_scr, acc_scr)
        return carry

    lax.fori_loop(0, nkt, attend_tile, 0)
    _softmax_finish(o_ref, l_scr, acc_scr)


def _dsa(dqa, iqa, iw_t, kk, vt, ik, *, t_len, kv_len):
    b, tp, _ = dqa.shape
    lp = kk.shape[1]
    tk = vt.shape[3]
    tq = min(256, tp)
    n_sel = min(TOPK_MAX, kv_len // 4)
    assert tk >= n_sel and lp % tk == 0 and tp % tq == 0
    nkt_max = lp // tk
    kern = functools.partial(_dsa_kernel, tq=tq, tk=tk, off=kv_len - t_len, kv_len=kv_len, n_sel=n_sel,
                             idx_bits=max(1, int(lp - 1).bit_length()))
    whole = lambda a: pl.BlockSpec((1,) + a.shape[1:], lambda bb, i: (bb,) + (0,) * (a.ndim - 1),
                                   pipeline_mode=pl.Buffered(1))
    return pl.pallas_call(
        kern,
        grid=(b, tp // tq),
        in_specs=[pl.BlockSpec((1, tq, QW), lambda bb, i: (bb, i, 0)),
                  pl.BlockSpec((1, tq, QW), lambda bb, i: (bb, i, 0)),
                  pl.BlockSpec((1, HEADS, tq), lambda bb, i: (bb, 0, i)),
                  whole(kk), whole(vt), whole(ik)],
        out_specs=pl.BlockSpec((1, tq, BRANCH_W), lambda bb, i: (bb, i, 0)),
        out_shape=jax.ShapeDtypeStruct((b, tp, BRANCH_W), F32),
        scratch_shapes=[pltpu.VMEM((nkt_max, tk, tq), jnp.int32), pltpu.VMEM((1, tq), jnp.int32),
                        pltpu.VMEM((HEADS, 1, tq), F32), pltpu.VMEM((HEADS, 1, tq), F32),
                        pltpu.VMEM((BRANCH_W, tq), F32)],
        compiler_params=_cparams("parallel", "arbitrary"),
        name="dsa",
    )(dqa, iqa, iw_t, kk, vt, ik)


def _merge_kernel(x_ref, oa_ref, ob_ref, oc_ref, gz_ref, wa_ref, wb_ref, wc_ref, wo_ref, o_ref):
    d = x_ref.shape[1]
    merged = jnp.zeros(x_ref.shape, F32)
    for n, (o_br, w_br) in enumerate(((oa_ref, wa_ref), (ob_ref, wb_ref), (oc_ref, wc_ref))):
        gate = jax.nn.sigmoid(gz_ref[:, n * d:(n + 1) * d])
        merged = merged + gate * jnp.dot(o_br[...].astype(BF16), w_br[...], preferred_element_type=F32)
    o_ref[...] = x_ref[...] + jnp.dot(merged.astype(BF16), wo_ref[...], preferred_element_type=F32)


def _merge(x, oa, ob, oc, z, wa, wb, wc, wo):
    m, d = x.shape
    tm = _row_tile(m)
    row = lambda w: pl.BlockSpec((tm, w), lambda i: (i, 0))
    full = lambda a: pl.BlockSpec(a.shape, lambda i: (0, 0))
    return pl.pallas_call(
        _merge_kernel,
        grid=(m // tm,),
        in_specs=[row(d), row(BRANCH_W), row(BRANCH_W), row(BRANCH_W), row(3 * d), full(wa), full(wb), full(wc), full(wo)],
        out_specs=row(d),
        out_shape=jax.ShapeDtypeStruct((m, d), F32),
        compiler_params=_cparams("parallel"),
        name="merge",
    )(x, oa, ob, oc, z, wa, wb, wc, wo)


def _ffn_kernel(x_ref, g_ref, wg_ref, wu_ref, wd_ref, o_ref, hn_scr, acc_scr):
    j = pl.program_id(1)

    @pl.when(j == 0)
    def _():
        x = x_ref[...]
        hn_scr[...] = (x * lax.rsqrt(jnp.mean(x * x, axis=-1, keepdims=True) + EPS) * g_ref[...]).astype(BF16)
        acc_scr[...] = jnp.zeros(acc_scr.shape, F32)

    hn = hn_scr[...]
    a = jnp.dot(hn, wg_ref[...], preferred_element_type=F32)
    u = jnp.dot(hn, wu_ref[...], preferred_element_type=F32)
    acc_scr[...] += jnp.dot((jax.nn.silu(a) * u).astype(BF16), wd_ref[...], preferred_element_type=F32)

    @pl.when(j == pl.num_programs(1) - 1)
    def _():
        o_ref[...] = x_ref[...] + acc_scr[...]


def _ffn(x, g, wg, wu, wd):
    m, d = x.shape
    dff = wg.shape[1]
    tm = _row_tile(m)
    tf = dff // 2 if (dff // 2) % LANES == 0 else dff
    return pl.pallas_call(
        _ffn_kernel,
        grid=(m // tm, dff // tf),
        in_specs=[pl.BlockSpec((tm, d), lambda i, j: (i, 0)),
                  pl.BlockSpec((1, d), lambda i, j: (0, 0)),
                  pl.BlockSpec((d, tf), lambda i, j: (0, j)),
                  pl.BlockSpec((d, tf), lambda i, j: (0, j)),
                  pl.BlockSpec((tf, d), lambda i, j: (j, 0))],
        out_specs=pl.BlockSpec((tm, d), lambda i, j: (i, 0)),
        out_shape=jax.ShapeDtypeStruct((m, d), F32),
        scratch_shapes=[pltpu.VMEM((tm, d), BF16), pltpu.VMEM((tm, d), F32)],
        compiler_params=_cparams("parallel", "arbitrary"),
        name="ffn",
    )(x, g.reshape(1, d).astype(F32), wg, wu, wd)


def _rmsnorm_kernel(x_ref, g_ref, o_ref):
    x = x_ref[...]
    o_ref[...] = x * lax.rsqrt(jnp.mean(x * x, axis=-1, keepdims=True) + EPS) * g_ref[...]


def _rmsnorm(x, g):
    m, d = x.shape
    tm = _row_tile(m)
    return pl.pallas_call(
        _rmsnorm_kernel,
        grid=(m // tm,),
        in_specs=[pl.BlockSpec((tm, d), lambda i: (i, 0)), pl.BlockSpec((1, d), lambda i: (0, 0))],
        out_specs=pl.BlockSpec((tm, d), lambda i: (i, 0)),
        out_shape=jax.ShapeDtypeStruct((m, d), F32),
        compiler_params=_cparams("parallel"),
        name="final_rmsnorm",
    )(x, g.reshape(1, d).astype(F32))


def _head_strided(w, lane_of_head=lambda h: 0):
    lead = w.shape[:-1]
    w = w.reshape(lead + (HEADS, HEAD_DIM))
    out = jnp.zeros(lead + (HEADS, HS), w.dtype)
    for h in range(HEADS):
        o = lane_of_head(h)
        out = out.at[..., h, o:o + HEAD_DIM].set(w[..., h, :])
    return out.reshape(lead + (QW,))


def _pack_weights(w_in, w_mla_uq, w_mla_ukv):
    sizes = (BRANCH_W, BRANCH_W, BRANCH_W, HEADS, 256, 128, MLA_ROPE, BRANCH_W, 128, 128, BRANCH_W, HEAD_DIM, HEADS)
    offs = np.concatenate([[0], np.cumsum(sizes)])
    (fq, fk, fv, ff, cq, ckv, kr, dq, dk, dv, iq, ik, iw) = [w_in[..., int(offs[n]):int(offs[n + 1])]
                                                            for n in range(len(sizes))]
    gz = w_in[..., int(offs[-1]):]
    pad = jnp.zeros(w_in.shape[:-1] + (LANES - (HEAD_DIM + MLA_ROPE + 2 * HEADS),), w_in.dtype)
    kv_rep = HEADS // DSA_KV_HEADS
    w_in_p = jnp.concatenate(
        [gz, _head_strided(fq), _head_strided(fk), _head_strided(dq, lambda h: (h // kv_rep) * HEAD_DIM),
         _head_strided(iq), fv, cq, ckv, dk, dv, ik, kr, ff, iw, pad], axis=-1).astype(BF16)
    assert w_in_p.shape[-1] == N_PACK

    depth = w_in.shape[0]
    uq = w_mla_uq.reshape(depth, w_mla_uq.shape[1], HEADS, HEAD_DIM + MLA_ROPE)
    uq = jnp.concatenate([uq, jnp.zeros(uq.shape[:-1] + (HS - HEAD_DIM - MLA_ROPE,), uq.dtype)], axis=-1)
    w_uq_p = uq.reshape(depth, -1, QW).astype(BF16)
    ukv = w_mla_ukv.reshape(depth, w_mla_ukv.shape[1], HEADS, 2 * HEAD_DIM)
    w_uk_p = _head_strided(ukv[..., :HEAD_DIM].reshape(depth, -1, BRANCH_W)).astype(BF16)
    w_uv_p = ukv[..., HEAD_DIM:].reshape(depth, -1, BRANCH_W).astype(BF16)
    return w_in_p, w_uq_p, w_uk_p, w_uv_p


def _with_past(past, new, lp):
    parts = [new] if past is None else [past.astype(new.dtype), new]
    rows = sum(p.shape[1] for p in parts)
    if lp > rows:
        parts.append(jnp.zeros((new.shape[0], lp - rows, new.shape[2]), new.dtype))
    return parts[0] if len(parts) == 1 else jnp.concatenate(parts, axis=1)


def _pad_axis(a, axis, size):
    if a.shape[axis] == size:
        return a
    widths = [(0, 0)] * a.ndim
    widths[axis] = (0, size - a.shape[axis])
    return jnp.pad(a, widths)


def _layer(x, past, tabs, p):
    b, t_len, d = x.shape
    m = b * t_len
    p_len = 0 if past is None else past[0].shape[1]
    kv_len = p_len + t_len
    tk = min(512, _round_up(kv_len, LANES))
    lp = _round_up(kv_len, tk)
    tp = _round_up(t_len, LANES)
    x2 = x.reshape(m, d)

    z = _rms_matmul(x2, 0, d, p["norm_attn"], p["w_in"], tn=N_PACK // 6)
    qm = _rms_matmul(z, C_CQ // 256, 256, p["norm_mla_q"], p["w_uq"], tn=256)
    fqa, dqa, iqa, mqa, ckv_n, dk_r, misc = _post(z, qm, tabs, p["b_forget"], p["norm_mla_kv"], t_len)

    z3 = z.reshape(b, t_len, N_PACK)
    r3 = lambda a: a.reshape(b, t_len, a.shape[-1])
    fk = z3[..., C_FK:C_FK + QW].reshape(b, t_len, HEADS, HS)[..., :HEAD_DIM]
    fv = z3[..., C_FV:C_FV + BRANCH_W]
    dv = z3[..., C_DV:C_DV + LANES]
    misc3, ckv3, dk3 = r3(misc), r3(ckv_n), r3(dk_r)
    logf = misc3[..., M_FF:M_FF + HEADS]
    new_state = (fk, fv.reshape(b, t_len, HEADS, HEAD_DIM), logf, ckv3, misc3[..., M_KR:M_KR + MLA_ROPE],
                 dk3.reshape(b, t_len, DSA_KV_HEADS, HEAD_DIM), dv.reshape(b, t_len, DSA_KV_HEADS, HEAD_DIM),
                 misc3[..., M_IK:M_IK + HEAD_DIM])

    if past is None:
        fox_k, fox_v = (z3, C_FK // QW, QW), (z3, C_FV // BRANCH_W, BRANCH_W)
        dsa_k, dsa_v = (dk3, 0, LANES), (z3, C_DV // LANES, LANES)
        logf_all, ckv_all, kpe_all = logf, ckv3, misc3
        ik_all = misc3.astype(BF16)
    else:
        (c_fk, c_fv, c_logf, c_ckv, c_kpe, c_dk, c_dv, c_ik) = past
        flat = lambda a: a.reshape(b, p_len, -1)
        c_fk_s = _pad_axis(c_fk, 3, HS).reshape(b, p_len, QW)
        fox_k = (_with_past(c_fk_s, z3[..., C_FK:C_FK + QW], lp), 0, QW)
        fox_v = (_with_past(flat(c_fv), fv, lp), 0, BRANCH_W)
        dsa_k = (_with_past(flat(c_dk), dk3, lp), 0, LANES)
        dsa_v = (_with_past(flat(c_dv), dv, lp), 0, LANES)
        logf_all = _with_past(c_logf, logf, lp)
        ckv_all = _with_past(c_ckv, ckv3, lp)
        c_kpe_s = jnp.pad(c_kpe, ((0, 0), (0, 0), (M_KR, LANES - M_KR - MLA_ROPE)))
        kpe_all = _with_past(c_kpe_s, misc3, lp)
        ik_all = _with_past(_pad_axis(c_ik, 2, LANES), misc3, lp).astype(BF16)

    pad_q = lambda a: _pad_axis(a.reshape(b, t_len, a.shape[-1]), 1, tp)

    cum = _cumsum(_pad_axis(logf_all, 1, lp)) * LOG2E
    q_bias = _pad_axis(cum[:, :, kv_len - t_len:kv_len], 2, tp)
    fka, fvt = _kside(fox_k, fox_v, tk)
    o_a = _flash(pad_q(fqa), fka, fvt, t_len=t_len, kv_len=kv_len, per_frame=True,
                 bias=(q_bias, cum.transpose(0, 2, 1)))

    mka, mvt = _mla_kv(_pad_axis(ckv_all, 1, lp), _pad_axis(kpe_all, 1, lp), p["w_uk"], p["w_uv"], tk)
    o_b = _flash(pad_q(mqa), mka, mvt, t_len=t_len, kv_len=kv_len, per_frame=False)

    dka, dvt = _kside(dsa_k, dsa_v, tk)
    iw_t = _pad_axis(misc3[..., M_IW:M_IW + HEADS].transpose(0, 2, 1), 2, tp)
    o_c = _dsa(pad_q(dqa), pad_q(iqa), iw_t, dka, dvt, _pad_axis(ik_all, 1, lp), t_len=t_len, kv_len=kv_len)

    unpad = lambda o: o[:, :t_len].reshape(m, BRANCH_W)
    x2 = _merge(x2, unpad(o_a), unpad(o_b), unpad(o_c), z,
                p["w_br_fox"], p["w_br_mla"], p["w_br_dsa"], p["w_out"])
    x2 = _ffn(x2, p["norm_ffn"], p["w_ffn_gate"], p["w_ffn_up"], p["w_ffn_down"])
    return x2.reshape(b, t_len, d), new_state


def kernel(x_prompt, x_sample, cache_fox_k, cache_fox_v, cache_fox_logf, cache_mla_ckv, cache_mla_kpe, cache_dsa_k, cache_dsa_v, cache_dsa_idxk, norm_attn, w_in, b_forget, norm_mla_q, norm_mla_kv, w_mla_uq, w_mla_ukv, w_br_fox, w_br_mla, w_br_dsa, w_out, norm_ffn, w_ffn_gate, w_ffn_up, w_ffn_down, norm_final):
    depth = w_in.shape[0]
    past_len = cache_fox_k.shape[2]
    w_in_p, w_uq_p, w_uk_p, w_uv_p = _pack_weights(w_in, w_mla_uq, w_mla_ukv)
    bf = lambda a: a.astype(BF16)
    w_br_fox, w_br_mla, w_br_dsa, w_out = bf(w_br_fox), bf(w_br_mla), bf(w_br_dsa), bf(w_out)
    w_ffn_gate, w_ffn_up, w_ffn_down = bf(w_ffn_gate), bf(w_ffn_up), bf(w_ffn_down)
    caches = (cache_fox_k, cache_fox_v, cache_fox_logf, cache_mla_ckv, cache_mla_kpe,
              cache_dsa_k, cache_dsa_v, cache_dsa_idxk)

    tabs_p = _rope_tables(jnp.arange(x_prompt.shape[1]))
    tabs_s = _rope_tables(past_len + jnp.arange(x_sample.shape[1]))

    xp, xs = x_prompt, x_sample
    states_p, states_s = [], []
    for l in range(depth):
        p = dict(norm_attn=norm_attn[l], w_in=w_in_p[l], b_forget=b_forget[l], norm_mla_q=norm_mla_q[l],
                 norm_mla_kv=norm_mla_kv[l], w_uq=w_uq_p[l], w_uk=w_uk_p[l], w_uv=w_uv_p[l], w_br_fox=w_br_fox[l],
                 w_br_mla=w_br_mla[l], w_br_dsa=w_br_dsa[l], w_out=w_out[l], norm_ffn=norm_ffn[l],
                 w_ffn_gate=w_ffn_gate[l], w_ffn_up=w_ffn_up[l], w_ffn_down=w_ffn_down[l])
        xp, st_p = _layer(xp, None, tabs_p, p)
        xs, st_s = _layer(xs, tuple(c[l] for c in caches), tabs_s, p)
        states_p.append(st_p)
        states_s.append(st_s)

    stacked_p = [jnp.stack(a, axis=0) for a in zip(*states_p)]
    stacked_s = [jnp.stack(a, axis=0) for a in zip(*states_s)]
    yp = _rmsnorm(xp.reshape(-1, xp.shape[-1]), norm_final).reshape(xp.shape)
    ys = _rmsnorm(xs.reshape(-1, xs.shape[-1]), norm_final).reshape(xs.shape)
    out = [yp, ys]
    for sp, ss in zip(stacked_p, stacked_s):
        out += [sp, ss]
    return tuple(out)
```

```python
import functools
import math

import numpy as np
import jax
import jax.numpy as jnp
from jax import lax
from jax.experimental import pallas as pl
from jax.experimental.pallas import tpu as pltpu

F32 = jnp.float32
BF16 = jnp.bfloat16

CHUNK = 64
EPS = 1e-6
HEADS = 8
HEAD_DIM = 64
MLA_ROPE = 32
MLA_THETA = 10000.0
DSA_KV_HEADS = 2
ROPE_THETA = 500000.0
PART_ROT = HEAD_DIM // 4
TOPK_MAX = 256
BRANCH_W = HEADS * HEAD_DIM

LANES = 128
SUBLANES = 8
VMEM_LIMIT = 56 * 1024 * 1024
NEG = -1e30
LOG2E = math.log2(math.e)
HS = LANES
QW = HEADS * HS
VR = HEAD_DIM + 16

C_GZ = 0
C_FQ = 3072
C_FK = 4096
C_DQ = 5120
C_IQ = 6144
C_FV = 7168
C_CQ = 7680
C_CKV = 7936
C_DK = 8064
C_DV = 8192
C_MISC = 8320
N_PACK = 8448
M_IK, M_KR, M_FF, M_IW = 0, 64, 96, 104

KEY_NEG_INF = int(np.array(0xFF800000 ^ 0x7FFFFFFF, dtype=np.uint32).view(np.int32))
KEY_POS_INF = 0x7F800000
INT_MIN = -(2 ** 31)
INT_MAX = 2 ** 31 - 1

NT_DIMS = (((1,), (1,)), ((), ()))


def _cparams(*sem):
    return pltpu.CompilerParams(dimension_semantics=sem, vmem_limit_bytes=VMEM_LIMIT)


def _round_up(a, b):
    return -(-a // b) * b


def _row_tile(m, cap=512):
    t = cap
    while t > 8 and m % t:
        t //= 2
    assert m % t == 0
    return t


def _rms_mm_kernel(x_ref, g_ref, w_ref, o_ref, xn_ref, *, normalize):
    @pl.when(pl.program_id(1) == 0)
    def _():
        x = x_ref[...].astype(F32)
        if normalize:
            x = x * lax.rsqrt(jnp.mean(x * x, axis=-1, keepdims=True) + EPS) * g_ref[...]
        xn_ref[...] = x.astype(BF16)

    o_ref[...] = jnp.dot(xn_ref[...], w_ref[...], preferred_element_type=F32).astype(o_ref.dtype)


def _rms_matmul(x, col_block, k, g, w, *, tn, out_dtype=F32):
    m = x.shape[0]
    n = w.shape[1]
    tm = _row_tile(m)
    normalize = g is not None
    if g is None:
        g = jnp.ones((k,), F32)
    return pl.pallas_call(
        functools.partial(_rms_mm_kernel, normalize=normalize),
        grid=(m // tm, n // tn),
        in_specs=[pl.BlockSpec((tm, k), lambda i, j: (i, col_block)),
                  pl.BlockSpec((1, k), lambda i, j: (0, 0)),
                  pl.BlockSpec((k, tn), lambda i, j: (0, j))],
        out_specs=pl.BlockSpec((tm, tn), lambda i, j: (i, j)),
        out_shape=jax.ShapeDtypeStruct((m, n), out_dtype),
        scratch_shapes=[pltpu.VMEM((tm, k), BF16)],
        compiler_params=_cparams("parallel", "arbitrary"),
        name="rms_matmul",
    )(x, g.reshape(1, k).astype(F32), w)


N_TABS = 12
TAB_IQ, TAB_DK, TAB_DQ_HI, TAB_MLA = 0, 3, 6, 9


def _rope_tables(pos):
    lane = np.arange(LANES)

    def pattern(theta, rot, period, lane0, width):
        half = rot // 2
        inv_freq = theta ** (-jnp.arange(half, dtype=F32) * (2.0 / rot))
        ang = pos.astype(F32)[:, None] * inv_freq[None, :]
        cos, sin = jnp.cos(ang), jnp.sin(ang)
        inside = (lane >= lane0) & (lane < lane0 + width)
        r = (lane - lane0) % period
        first = inside & (r < half)
        second = inside & (r >= half) & (r < rot)
        fidx = np.where(first, r, np.where(second, r - half, 0))
        c = jnp.where((first | second)[None, :], cos[:, fidx], 1.0)
        s1 = jnp.where(first[None, :], -sin[:, fidx], 0.0)
        s2 = jnp.where(second[None, :], sin[:, fidx], 0.0)
        return [c, s1, s2]

    tabs = (pattern(ROPE_THETA, PART_ROT, HS, 0, HS)
            + pattern(ROPE_THETA, PART_ROT, HEAD_DIM, 0, LANES)
            + pattern(ROPE_THETA, PART_ROT, HEAD_DIM, HEAD_DIM, HEAD_DIM)
            + pattern(MLA_THETA, MLA_ROPE, MLA_ROPE, M_KR, MLA_ROPE))
    return jnp.stack(tabs, axis=0)


def _rot(x, tab_ref, t0, half):
    return (x * tab_ref[t0] + pltpu.roll(x, LANES - half, 1) * tab_ref[t0 + 1]
            + pltpu.roll(x, half, 1) * tab_ref[t0 + 2])


def _post_kernel(fq_ref, dq_ref, iq_ref, ckv_ref, dk_ref, misc_ref, qm_ref, tab_ref, bias_ref, gkv_ref,
                 fqa_o, dqa_o, iqa_o, mqa_o, ckv_o, dk_o, misc_o):
    half = PART_ROT // 2
    kv_rep = HEADS // DSA_KV_HEADS
    sc_dot = (HEAD_DIM ** -0.5) * LOG2E
    sc_mla = ((HEAD_DIM + MLA_ROPE) ** -0.5) * LOG2E
    for h in range(HEADS):
        sl = slice(h * HS, (h + 1) * HS)
        fqa_o[:, sl] = (fq_ref[:, sl] * sc_dot).astype(BF16)
        t_dq = TAB_IQ if h < kv_rep else TAB_DQ_HI
        dqa_o[:, sl] = (_rot(dq_ref[:, sl], tab_ref, t_dq, half) * sc_dot).astype(BF16)
        iqa_o[:, sl] = _rot(iq_ref[:, sl], tab_ref, TAB_IQ, half).astype(BF16)
        mqa_o[:, sl] = (_rot(qm_ref[:, sl], tab_ref, TAB_MLA, MLA_ROPE // 2) * sc_mla).astype(BF16)
    dk_o[...] = _rot(dk_ref[...], tab_ref, TAB_DK, half)

    ckv = ckv_ref[...]
    ckv_o[...] = ckv * lax.rsqrt(jnp.mean(ckv * ckv, axis=-1, keepdims=True) + EPS) * gkv_ref[...]

    x = misc_ref[...]
    roped = (x * (tab_ref[TAB_IQ] * tab_ref[TAB_MLA])
             + pltpu.roll(x, LANES - half, 1) * tab_ref[TAB_IQ + 1] + pltpu.roll(x, half, 1) * tab_ref[TAB_IQ + 2]
             + pltpu.roll(x, LANES - MLA_ROPE // 2, 1) * tab_ref[TAB_MLA + 1]
             + pltpu.roll(x, MLA_ROPE // 2, 1) * tab_ref[TAB_MLA + 2])
    lane = lax.broadcasted_iota(jnp.int32, x.shape, 1)
    logf = jax.nn.log_sigmoid(x + bias_ref[...])
    is_ff = jnp.where(lane >= M_FF, jnp.where(lane < M_IW, 1, 0), 0)
    misc_o[...] = jnp.where(is_ff == 1, logf, roped)


def _post(z, qm, tabs, b_forget, g_kv, t_len):
    m = z.shape[0]
    tm = _row_tile(m)
    if t_len % tm == 0:
        nt = t_len // tm
        tab_map = lambda i: (0, i % nt, 0)
    else:
        assert tm % t_len == 0
        tabs = jnp.tile(tabs, (1, tm // t_len, 1))
        tab_map = lambda i: (0, 0, 0)
    bias = jnp.zeros((1, LANES), F32).at[0, M_FF:M_FF + HEADS].set(b_forget.astype(F32))
    blk = lambda w, c0: pl.BlockSpec((tm, w), lambda i: (i, c0 // w))
    outs = [(QW, BF16), (QW, BF16), (QW, BF16), (QW, BF16), (LANES, F32), (LANES, F32), (LANES, F32)]
    return pl.pallas_call(
        _post_kernel,
        grid=(m // tm,),
        in_specs=[blk(QW, C_FQ), blk(QW, C_DQ), blk(QW, C_IQ), blk(LANES, C_CKV), blk(LANES, C_DK),
                  blk(LANES, C_MISC), pl.BlockSpec((tm, QW), lambda i: (i, 0)),
                  pl.BlockSpec((N_TABS, tm, LANES), tab_map),
                  pl.BlockSpec((1, LANES), lambda i: (0, 0)),
                  pl.BlockSpec((1, LANES), lambda i: (0, 0))],
        out_specs=[pl.BlockSpec((tm, w), lambda i: (i, 0)) for w, _ in outs],
        out_shape=[jax.ShapeDtypeStruct((m, w), dt) for w, dt in outs],
        compiler_params=_cparams("parallel"),
        name="post",
    )(z, z, z, z, z, z, qm, tabs, bias, g_kv.reshape(1, LANES).astype(F32))


def _cumsum_kernel(x_ref, o_ref):
    nc = x_ref.shape[1]
    row = lax.broadcasted_iota(jnp.int32, (LANES, LANES), 0)
    col = lax.broadcasted_iota(jnp.int32, (LANES, LANES), 1)
    tri = jnp.where(row <= col, 1.0, 0.0).astype(F32)

    def body(c, carry):
        cs = jnp.dot(x_ref[0, c], tri, precision=lax.Precision.HIGHEST, preferred_element_type=F32) + carry
        o_ref[0, c] = cs
        return cs[:, LANES - 1:LANES]

    lax.fori_loop(0, nc, body, jnp.zeros((HEADS, 1), F32))


def _cumsum(logf):
    b, lp, h = logf.shape
    nc = lp // LANES
    x = logf.transpose(0, 2, 1).reshape(b, h, nc, LANES).transpose(0, 2, 1, 3)
    out = pl.pallas_call(
        _cumsum_kernel,
        grid=(b,),
        in_specs=[pl.BlockSpec((1, nc, h, LANES), lambda i: (i, 0, 0, 0))],
        out_specs=pl.BlockSpec((1, nc, h, LANES), lambda i: (i, 0, 0, 0)),
        out_shape=jax.ShapeDtypeStruct((b, nc, h, LANES), F32),
        compiler_params=_cparams("parallel"),
        name="cumsum",
    )(x)
    return out.transpose(0, 2, 1, 3).reshape(b, h, lp)


def _store_value_tiles(vt_o, v):
    vt = v.T
    tk = v.shape[0]
    for h in range(v.shape[1] // HEAD_DIM):
        vt_o[0, 0, h * VR:h * VR + HEAD_DIM, :] = vt[h * HEAD_DIM:(h + 1) * HEAD_DIM, :].astype(BF16)
        vt_o[0, 0, h * VR + HEAD_DIM:(h + 1) * VR, :] = jnp.ones((VR - HEAD_DIM, tk), BF16)


def _kside_kernel(k_ref, v_ref, ka_o, vt_o):
    ka_o[0] = k_ref[0].astype(BF16)
    _store_value_tiles(vt_o, v_ref[0])


def _kside(k, v, tk):
    (ka, kcb, kw), (va, vcb, vw) = k, v
    b, lp = ka.shape[0], ka.shape[1]
    nkt = lp // tk
    vr = vw // HEAD_DIM * VR
    return pl.pallas_call(
        _kside_kernel,
        grid=(b, nkt),
        in_specs=[pl.BlockSpec((1, tk, kw), lambda bb, j: (bb, j, kcb)),
                  pl.BlockSpec((1, tk, vw), lambda bb, j: (bb, j, vcb))],
        out_specs=[pl.BlockSpec((1, tk, kw), lambda bb, j: (bb, j, 0)),
                   pl.BlockSpec((1, 1, vr, tk), lambda bb, j: (bb, j, 0, 0))],
        out_shape=[jax.ShapeDtypeStruct((b, lp, kw), BF16), jax.ShapeDtypeStruct((b, nkt, vr, tk), BF16)],
        compiler_params=_cparams("parallel", "parallel"),
        name="kside",
    )(ka, va)


def _mla_kv_kernel(ckv_ref, kpe_ref, wk_ref, wv_ref, ka_o, vt_o):
    c = ckv_ref[0].astype(BF16)
    k = jnp.dot(c, wk_ref[...], preferred_element_type=F32)
    kpe = kpe_ref[0]
    lane = lax.broadcasted_iota(jnp.int32, kpe.shape, 1)
    kpe = jnp.where(lane >= M_KR, jnp.where(lane < M_KR + MLA_ROPE, kpe, 0.0), 0.0)
    for h in range(HEADS):
        sl = slice(h * HS, (h + 1) * HS)
        ka_o[0, :, sl] = (k[:, sl] + kpe).astype(BF16)
    _store_value_tiles(vt_o, jnp.dot(c, wv_ref[...], preferred_element_type=F32))


def _mla_kv(ckv_all, kpe_all, wk, wv, tk):
    b, lp, _ = ckv_all.shape
    nkt = lp // tk
    return pl.pallas_call(
        _mla_kv_kernel,
        grid=(b, nkt),
        in_specs=[pl.BlockSpec((1, tk, LANES), lambda bb, j: (bb, j, 0)),
                  pl.BlockSpec((1, tk, LANES), lambda bb, j: (bb, j, 0)),
                  pl.BlockSpec(wk.shape, lambda bb, j: (0, 0)),
                  pl.BlockSpec(wv.shape, lambda bb, j: (0, 0))],
        out_specs=[pl.BlockSpec((1, tk, QW), lambda bb, j: (bb, j, 0)),
                   pl.BlockSpec((1, 1, HEADS * VR, tk), lambda bb, j: (bb, j, 0, 0))],
        out_shape=[jax.ShapeDtypeStruct((b, lp, QW), BF16), jax.ShapeDtypeStruct((b, nkt, HEADS * VR, tk), BF16)],
        compiler_params=_cparams("parallel", "parallel"),
        name="mla_kv",
    )(ckv_all, kpe_all, wk, wv)


def _attend_tile(score_fn, vt_fn, m_scr, acc_scr, s_scr, tq):
    for h in range(HEADS):
        s_scr[:, h * tq:(h + 1) * tq] = score_fn(h)
    s_all = s_scr[...]
    m_prev = m_scr[...]
    m_new = jnp.maximum(m_prev, jnp.max(s_all, axis=0, keepdims=True))
    alpha = jnp.exp2(m_prev - m_new)
    m_scr[...] = m_new
    p_all = jnp.exp2((s_all - m_new).astype(BF16))
    for h in range(HEADS):
        rows = slice(h * VR, (h + 1) * VR)
        cols = slice(h * tq, (h + 1) * tq)
        acc_scr[rows, :] = alpha[:, cols] * acc_scr[rows, :] + jnp.dot(vt_fn(h), p_all[:, cols],
                                                                     preferred_element_type=F32)


def _softmax_init(m_scr, acc_scr):
    m_scr[...] = jnp.full(m_scr.shape, NEG, F32)
    acc_scr[...] = jnp.zeros(acc_scr.shape, F32)


def _softmax_finish(o_ref, acc_scr):
    outs = [acc_scr[h * VR:h * VR + HEAD_DIM, :] / acc_scr[h * VR + HEAD_DIM:h * VR + HEAD_DIM + 1, :]
            for h in range(HEADS)]
    o_ref[0] = jnp.concatenate(outs, axis=0).T


def _last_allowed(q_pos, per_frame):
    return q_pos if per_frame else (q_pos | (CHUNK - 1))


def _flash_kernel(*refs, tq, tk, off, kv_len, per_frame, has_bias):
    if has_bias:
        qa_ref, ka_ref, vt_ref, qb_ref, kb_ref, o_ref, m_scr, acc_scr, s_scr = refs
    else:
        qa_ref, ka_ref, vt_ref, o_ref, m_scr, acc_scr, s_scr = refs
    i, j = pl.program_id(1), pl.program_id(2)
    q0 = i * tq + off
    k0 = j * tk
    k_last = k0 + tk - 1

    @pl.when(j == 0)
    def _():
        _softmax_init(m_scr, acc_scr)

    def step(masked):
        if masked:
            kpos = k0 + lax.broadcasted_iota(jnp.int32, (tk, tq), 0)
            qlim = _last_allowed(q0 + lax.broadcasted_iota(jnp.int32, (tk, tq), 1), per_frame)
            mask_bias = jnp.where(kpos <= qlim, jnp.where(kpos < kv_len, 0.0, NEG), NEG)

        def score(h):
            sl = slice(h * HS, (h + 1) * HS)
            s_t = lax.dot_general(ka_ref[0, :, sl], qa_ref[0, :, sl], NT_DIMS, preferred_element_type=F32)
            if has_bias:
                s_t = s_t + (qb_ref[0, h:h + 1, :] - kb_ref[0, :, h:h + 1])
            return s_t + mask_bias if masked else s_t

        _attend_tile(score, lambda h: vt_ref[0, 0, h * VR:(h + 1) * VR, :], m_scr, acc_scr, s_scr, tq)

    clear = jnp.logical_and(k_last <= _last_allowed(q0, per_frame), k_last < kv_len)
    needed = k0 <= _last_allowed(q0 + tq - 1, per_frame)

    @pl.when(clear)
    def _():
        step(False)

    @pl.when(jnp.logical_and(needed, jnp.logical_not(clear)))
    def _():
        step(True)

    @pl.when(j == pl.num_programs(2) - 1)
    def _():
        _softmax_finish(o_ref, acc_scr)


def _flash(qa, ka, vt, *, t_len, kv_len, per_frame, bias=None):
    b, tp, _ = qa.shape
    lp = ka.shape[1]
    tk = vt.shape[3]
    tq = min(512, tp)
    off = kv_len - t_len
    nq, nk = tp // tq, lp // tk

    def kidx(i, j):
        last = _last_allowed(i * tq + off + tq - 1, per_frame) // tk
        return jnp.minimum(j, jnp.minimum(last, nk - 1))

    operands = [qa, ka, vt]
    in_specs = [pl.BlockSpec((1, tq, QW), lambda bb, i, j: (bb, i, 0)),
                pl.BlockSpec((1, tk, QW), lambda bb, i, j: (bb, kidx(i, j), 0)),
                pl.BlockSpec((1, 1, HEADS * VR, tk), lambda bb, i, j: (bb, kidx(i, j), 0, 0))]
    if bias is not None:
        operands += list(bias)
        in_specs += [pl.BlockSpec((1, HEADS, tq), lambda bb, i, j: (bb, 0, i)),
                     pl.BlockSpec((1, tk, HEADS), lambda bb, i, j: (bb, kidx(i, j), 0))]
    kern = functools.partial(_flash_kernel, tq=tq, tk=tk, off=off, kv_len=kv_len, per_frame=per_frame,
                             has_bias=bias is not None)
    return pl.pallas_call(
        kern,
        grid=(b, nq, nk),
        in_specs=in_specs,
        out_specs=pl.BlockSpec((1, tq, BRANCH_W), lambda bb, i, j: (bb, i, 0)),
        out_shape=jax.ShapeDtypeStruct((b, tp, BRANCH_W), F32),
        scratch_shapes=[pltpu.VMEM((1, HEADS * tq), F32), pltpu.VMEM((HEADS * VR, tq), F32),
                        pltpu.VMEM((tk, HEADS * tq), F32)],
        compiler_params=_cparams("parallel", "parallel", "arbitrary"),
        name="flash_fox" if per_frame else "flash_mla",
    )(*operands)


def _sortable(x):
    bits = pltpu.bitcast(x, jnp.int32)
    return jnp.where(bits < 0, bits ^ INT_MAX, bits)


def _dsa_kernel(dqa_ref, iqa_ref, iw_ref, kk_ref, vt_ref, ik_ref, o_ref, key_scr, j_scr, m_scr, acc_scr, s_scr,
                *, tq, tk, off, kv_len, n_sel, idx_bits):
    i = pl.program_id(1)
    q0 = i * tq + off
    qlim = (q0 + lax.broadcasted_iota(jnp.int32, (1, tq), 1)) | (CHUNK - 1)
    k_end = jnp.minimum(((q0 + tq - 1) | (CHUNK - 1)) + 1, kv_len)
    nkt = (k_end + tk - 1) // tk
    kv_rep = HEADS // DSA_KV_HEADS

    def score_tile(j, carry):
        ks = pl.multiple_of(j * tk, tk)
        ik = ik_ref[0, pl.ds(ks, tk), :]
        sc = jnp.zeros((tk, tq), F32)
        for h in range(HEADS):
            s = lax.dot_general(ik, iqa_ref[0, :, h * HS:(h + 1) * HS], NT_DIMS, preferred_element_type=F32)
            sc = sc + jnp.maximum(s, 0.0) * iw_ref[0, h:h + 1, :]
        kpos = ks + lax.broadcasted_iota(jnp.int32, (tk, tq), 0)
        adm = jnp.where(kpos <= qlim, jnp.where(kpos < kv_len, 1, 0), 0)
        key_scr[j] = jnp.where(adm == 1, _sortable(sc), KEY_NEG_INF)
        return carry

    lax.fori_loop(0, nkt, score_tile, 0)

    def count(pred):
        def body(j, acc):
            ind = pred(key_scr[j], j)
            return acc + jnp.sum(ind.reshape(tk // SUBLANES, SUBLANES, tq), axis=0)

        acc = lax.fori_loop(0, nkt, body, jnp.zeros((SUBLANES, tq), jnp.int32))
        return jnp.sum(acc, axis=0, keepdims=True)

    def count_ge(cand):
        return count(lambda kt, j: jnp.where(kt >= cand, 1, 0))

    def bit_step(s, ans):
        cand = ans + jnp.left_shift(jnp.int32(1), 31 - s)
        return jnp.where(count_ge(cand) >= n_sel, cand, ans)

    thr = lax.fori_loop(0, 32, bit_step, jnp.full((1, tq), INT_MIN, jnp.int32))

    c_gt = count(lambda kt, j: jnp.where(kt > thr, 1, 0))
    c_eq = count(lambda kt, j: jnp.where(kt == thr, 1, 0))
    need = n_sel - c_gt
    j_scr[...] = jnp.full((1, tq), INT_MAX, jnp.int32)

    @pl.when(jnp.max(c_eq - need) > 0)
    def _():
        def idx_step(s, jv):
            cand = jv + jnp.left_shift(jnp.int32(1), idx_bits - 1 - s)

            def pred(kt, j):
                idx = j * tk + lax.broadcasted_iota(jnp.int32, (tk, tq), 0)
                return jnp.where(kt == thr, jnp.where(idx < cand, 1, 0), 0)

            return jnp.where(count(pred) < need, cand, jv)

        j_scr[...] = lax.fori_loop(0, idx_bits, idx_step, jnp.zeros((1, tq), jnp.int32))

    short = thr <= KEY_NEG_INF
    lo = jnp.where(short, KEY_NEG_INF, thr)
    j_lim = jnp.where(short, -1, j_scr[...])

    _softmax_init(m_scr, acc_scr)

    def attend_tile(j, carry):
        ks = pl.multiple_of(j * tk, tk)
        kt = key_scr[j]
        idx = ks + lax.broadcasted_iota(jnp.int32, (tk, tq), 0)
        sel = jnp.where(kt > lo, 1, jnp.where(kt == lo, jnp.where(idx <= j_lim, 1, 0), 0))
        sel = jnp.where(kt < KEY_POS_INF, sel, 0)
        mask_bias = jnp.where(sel == 1, 0.0, NEG)
        kk = kk_ref[0, pl.ds(ks, tk), :]

        def score(h):
            return lax.dot_general(kk, dqa_ref[0, :, h * HS:(h + 1) * HS], NT_DIMS,
                                   preferred_element_type=F32) + mask_bias

        _attend_tile(score, lambda h: vt_ref[0, j, (h // kv_rep) * VR:(h // kv_rep + 1) * VR, :],
                     m_scr, acc_scr, s_scr, tq)
        return carry

    lax.fori_loop(0, nkt, attend_tile, 0)
    _softmax_finish(o_ref, acc_scr)


def _dsa(dqa, iqa, iw_t, kk, vt, ik, *, t_len, kv_len):
    b, tp, _ = dqa.shape
    lp = kk.shape[1]
    tk = vt.shape[3]
    tq = min(256, tp)
    n_sel = min(TOPK_MAX, kv_len // 4)
    assert tk >= n_sel and lp % tk == 0 and tp % tq == 0
    nkt_max = lp // tk
    kern = functools.partial(_dsa_kernel, tq=tq, tk=tk, off=kv_len - t_len, kv_len=kv_len, n_sel=n_sel,
                             idx_bits=max(1, int(lp - 1).bit_length()))
    whole = lambda a: pl.BlockSpec((1,) + a.shape[1:], lambda bb, i: (bb,) + (0,) * (a.ndim - 1),
                                   pipeline_mode=pl.Buffered(1))
    return pl.pallas_call(
        kern,
        grid=(b, tp // tq),
        in_specs=[pl.BlockSpec((1, tq, QW), lambda bb, i: (bb, i, 0)),
                  pl.BlockSpec((1, tq, QW), lambda bb, i: (bb, i, 0)),
                  pl.BlockSpec((1, HEADS, tq), lambda bb, i: (bb, 0, i)),
                  whole(kk), whole(vt), whole(ik)],
        out_specs=pl.BlockSpec((1, tq, BRANCH_W), lambda bb, i: (bb, i, 0)),
        out_shape=jax.ShapeDtypeStruct((b, tp, BRANCH_W), F32),
        scratch_shapes=[pltpu.VMEM((nkt_max, tk, tq), jnp.int32), pltpu.VMEM((1, tq), jnp.int32),
                        pltpu.VMEM((1, HEADS * tq), F32), pltpu.VMEM((HEADS * VR, tq), F32),
                        pltpu.VMEM((tk, HEADS * tq), F32)],
        compiler_params=_cparams("parallel", "arbitrary"),
        name="dsa",
    )(dqa, iqa, iw_t, kk, vt, ik)


def _merge_kernel(x_ref, oa_ref, ob_ref, oc_ref, gz_ref, wa_ref, wb_ref, wc_ref, wo_ref, o_ref):
    d = x_ref.shape[1]
    merged = jnp.zeros(x_ref.shape, F32)
    for n, (o_br, w_br) in enumerate(((oa_ref, wa_ref), (ob_ref, wb_ref), (oc_ref, wc_ref))):
        gate = jax.nn.sigmoid(gz_ref[:, n * d:(n + 1) * d])
        merged = merged + gate * jnp.dot(o_br[...].astype(BF16), w_br[...], preferred_element_type=F32)
    o_ref[...] = x_ref[...] + jnp.dot(merged.astype(BF16), wo_ref[...], preferred_element_type=F32)


def _merge(x, oa, ob, oc, z, wa, wb, wc, wo):
    m, d = x.shape
    tm = _row_tile(m)
    row = lambda w: pl.BlockSpec((tm, w), lambda i: (i, 0))
    full = lambda a: pl.BlockSpec(a.shape, lambda i: (0, 0))
    return pl.pallas_call(
        _merge_kernel,
        grid=(m // tm,),
        in_specs=[row(d), row(BRANCH_W), row(BRANCH_W), row(BRANCH_W), row(3 * d), full(wa), full(wb), full(wc), full(wo)],
        out_specs=row(d),
        out_shape=jax.ShapeDtypeStruct((m, d), F32),
        compiler_params=_cparams("parallel"),
        name="merge",
    )(x, oa, ob, oc, z, wa, wb, wc, wo)


def _ffn_kernel(x_ref, g_ref, wg_ref, wu_ref, wd_ref, o_ref, hn_scr, acc_scr):
    j = pl.program_id(1)

    @pl.when(j == 0)
    def _():
        x = x_ref[...]
        hn_scr[...] = (x * lax.rsqrt(jnp.mean(x * x, axis=-1, keepdims=True) + EPS) * g_ref[...]).astype(BF16)
        acc_scr[...] = jnp.zeros(acc_scr.shape, F32)

    hn = hn_scr[...]
    a = jnp.dot(hn, wg_ref[...], preferred_element_type=F32)
    u = jnp.dot(hn, wu_ref[...], preferred_element_type=F32)
    acc_scr[...] += jnp.dot((jax.nn.silu(a) * u).astype(BF16), wd_ref[...], preferred_element_type=F32)

    @pl.when(j == pl.num_programs(1) - 1)
    def _():
        o_ref[...] = x_ref[...] + acc_scr[...]


def _ffn(x, g, wg, wu, wd):
    m, d = x.shape
    dff = wg.shape[1]
    tm = _row_tile(m)
    tf = dff // 2 if (dff // 2) % LANES == 0 else dff
    return pl.pallas_call(
        _ffn_kernel,
        grid=(m // tm, dff // tf),
        in_specs=[pl.BlockSpec((tm, d), lambda i, j: (i, 0)),
                  pl.BlockSpec((1, d), lambda i, j: (0, 0)),
                  pl.BlockSpec((d, tf), lambda i, j: (0, j)),
                  pl.BlockSpec((d, tf), lambda i, j: (0, j)),
                  pl.BlockSpec((tf, d), lambda i, j: (j, 0))],
        out_specs=pl.BlockSpec((tm, d), lambda i, j: (i, 0)),
        out_shape=jax.ShapeDtypeStruct((m, d), F32),
        scratch_shapes=[pltpu.VMEM((tm, d), BF16), pltpu.VMEM((tm, d), F32)],
        compiler_params=_cparams("parallel", "arbitrary"),
        name="ffn",
    )(x, g.reshape(1, d).astype(F32), wg, wu, wd)


def _rmsnorm_kernel(x_ref, g_ref, o_ref):
    x = x_ref[...]
    o_ref[...] = x * lax.rsqrt(jnp.mean(x * x, axis=-1, keepdims=True) + EPS) * g_ref[...]


def _rmsnorm(x, g):
    m, d = x.shape
    tm = _row_tile(m)
    return pl.pallas_call(
        _rmsnorm_kernel,
        grid=(m // tm,),
        in_specs=[pl.BlockSpec((tm, d), lambda i: (i, 0)), pl.BlockSpec((1, d), lambda i: (0, 0))],
        out_specs=pl.BlockSpec((tm, d), lambda i: (i, 0)),
        out_shape=jax.ShapeDtypeStruct((m, d), F32),
        compiler_params=_cparams("parallel"),
        name="final_rmsnorm",
    )(x, g.reshape(1, d).astype(F32))


def _head_strided(w, lane_of_head=lambda h: 0):
    lead = w.shape[:-1]
    w = w.reshape(lead + (HEADS, HEAD_DIM))
    out = jnp.zeros(lead + (HEADS, HS), w.dtype)
    for h in range(HEADS):
        o = lane_of_head(h)
        out = out.at[..., h, o:o + HEAD_DIM].set(w[..., h, :])
    return out.reshape(lead + (QW,))


def _pack_weights(w_in, w_mla_uq, w_mla_ukv):
    sizes = (BRANCH_W, BRANCH_W, BRANCH_W, HEADS, 256, 128, MLA_ROPE, BRANCH_W, 128, 128, BRANCH_W, HEAD_DIM, HEADS)
    offs = np.concatenate([[0], np.cumsum(sizes)])
    (fq, fk, fv, ff, cq, ckv, kr, dq, dk, dv, iq, ik, iw) = [w_in[..., int(offs[n]):int(offs[n + 1])]
                                                            for n in range(len(sizes))]
    gz = w_in[..., int(offs[-1]):]
    pad = jnp.zeros(w_in.shape[:-1] + (LANES - (HEAD_DIM + MLA_ROPE + 2 * HEADS),), w_in.dtype)
    kv_rep = HEADS // DSA_KV_HEADS
    w_in_p = jnp.concatenate(
        [gz, _head_strided(fq), _head_strided(fk), _head_strided(dq, lambda h: (h // kv_rep) * HEAD_DIM),
         _head_strided(iq), fv, cq, ckv, dk, dv, ik, kr, ff, iw, pad], axis=-1).astype(BF16)
    assert w_in_p.shape[-1] == N_PACK

    depth = w_in.shape[0]
    uq = w_mla_uq.reshape(depth, w_mla_uq.shape[1], HEADS, HEAD_DIM + MLA_ROPE)
    uq = jnp.concatenate([uq, jnp.zeros(uq.shape[:-1] + (HS - HEAD_DIM - MLA_ROPE,), uq.dtype)], axis=-1)
    w_uq_p = uq.reshape(depth, -1, QW).astype(BF16)
    ukv = w_mla_ukv.reshape(depth, w_mla_ukv.shape[1], HEADS, 2 * HEAD_DIM)
    w_uk_p = _head_strided(ukv[..., :HEAD_DIM].reshape(depth, -1, BRANCH_W)).astype(BF16)
    w_uv_p = ukv[..., HEAD_DIM:].reshape(depth, -1, BRANCH_W).astype(BF16)
    return w_in_p, w_uq_p, w_uk_p, w_uv_p


def _with_past(past, new, lp):
    parts = [new] if past is None else [past.astype(new.dtype), new]
    rows = sum(p.shape[1] for p in parts)
    if lp > rows:
        parts.append(jnp.zeros((new.shape[0], lp - rows, new.shape[2]), new.dtype))
    return parts[0] if len(parts) == 1 else jnp.concatenate(parts, axis=1)


def _pad_axis(a, axis, size):
    if a.shape[axis] == size:
        return a
    widths = [(0, 0)] * a.ndim
    widths[axis] = (0, size - a.shape[axis])
    return jnp.pad(a, widths)


def _layer(x, past, tabs, p):
    b, t_len, d = x.shape
    m = b * t_len
    p_len = 0 if past is None else past[0].shape[1]
    kv_len = p_len + t_len
    tk = min(512, _round_up(kv_len, LANES))
    lp = _round_up(kv_len, tk)
    tp = _round_up(t_len, LANES)
    x2 = x.reshape(m, d)

    z = _rms_matmul(x2, 0, d, p["norm_attn"], p["w_in"], tn=N_PACK // 6)
    qm = _rms_matmul(z, C_CQ // 256, 256, p["norm_mla_q"], p["w_uq"], tn=256)
    fqa, dqa, iqa, mqa, ckv_n, dk_r, misc = _post(z, qm, tabs, p["b_forget"], p["norm_mla_kv"], t_len)

    z3 = z.reshape(b, t_len, N_PACK)
    r3 = lambda a: a.reshape(b, t_len, a.shape[-1])
    fk = z3[..., C_FK:C_FK + QW].reshape(b, t_len, HEADS, HS)[..., :HEAD_DIM]
    fv = z3[..., C_FV:C_FV + BRANCH_W]
    dv = z3[..., C_DV:C_DV + LANES]
    misc3, ckv3, dk3 = r3(misc), r3(ckv_n), r3(dk_r)
    logf = misc3[..., M_FF:M_FF + HEADS]
    new_state = (fk, fv.reshape(b, t_len, HEADS, HEAD_DIM), logf, ckv3, misc3[..., M_KR:M_KR + MLA_ROPE],
                 dk3.reshape(b, t_len, DSA_KV_HEADS, HEAD_DIM), dv.reshape(b, t_len, DSA_KV_HEADS, HEAD_DIM),
                 misc3[..., M_IK:M_IK + HEAD_DIM])

    if past is None:
        fox_k, fox_v = (z3, C_FK // QW, QW), (z3, C_FV // BRANCH_W, BRANCH_W)
        dsa_k, dsa_v = (dk3, 0, LANES), (z3, C_DV // LANES, LANES)
        logf_all, ckv_all, kpe_all = logf, ckv3, misc3
        ik_all = misc3.astype(BF16)
    else:
        (c_fk, c_fv, c_logf, c_ckv, c_kpe, c_dk, c_dv, c_ik) = past
        flat = lambda a: a.reshape(b, p_len, -1)
        c_fk_s = _pad_axis(c_fk, 3, HS).reshape(b, p_len, QW)
        fox_k = (_with_past(c_fk_s, z3[..., C_FK:C_FK + QW], lp), 0, QW)
        fox_v = (_with_past(flat(c_fv), fv, lp), 0, BRANCH_W)
        dsa_k = (_with_past(flat(c_dk), dk3, lp), 0, LANES)
        dsa_v = (_with_past(flat(c_dv), dv, lp), 0, LANES)
        logf_all = _with_past(c_logf, logf, lp)
        ckv_all = _with_past(c_ckv, ckv3, lp)
        c_kpe_s = jnp.pad(c_kpe, ((0, 0), (0, 0), (M_KR, LANES - M_KR - MLA_ROPE)))
        kpe_all = _with_past(c_kpe_s, misc3, lp)
        ik_all = _with_past(_pad_axis(c_ik, 2, LANES), misc3, lp).astype(BF16)

    pad_q = lambda a: _pad_axis(a.reshape(b, t_len, a.shape[-1]), 1, tp)

    cum = _cumsum(_pad_axis(logf_all, 1, lp)) * LOG2E
    q_bias = _pad_axis(cum[:, :, kv_len - t_len:kv_len], 2, tp)
    fka, fvt = _kside(fox_k, fox_v, tk)
    o_a = _flash(pad_q(fqa), fka, fvt, t_len=t_len, kv_len=kv_len, per_frame=True,
                 bias=(q_bias, cum.transpose(0, 2, 1)))

    mka, mvt = _mla_kv(_pad_axis(ckv_all, 1, lp), _pad_axis(kpe_all, 1, lp), p["w_uk"], p["w_uv"], tk)
    o_b = _flash(pad_q(mqa), mka, mvt, t_len=t_len, kv_len=kv_len, per_frame=False)

    dka, dvt = _kside(dsa_k, dsa_v, tk)
    iw_t = _pad_axis(misc3[..., M_IW:M_IW + HEADS].transpose(0, 2, 1), 2, tp)
    o_c = _dsa(pad_q(dqa), pad_q(iqa), iw_t, dka, dvt, _pad_axis(ik_all, 1, lp), t_len=t_len, kv_len=kv_len)

    unpad = lambda o: o[:, :t_len].reshape(m, BRANCH_W)
    x2 = _merge(x2, unpad(o_a), unpad(o_b), unpad(o_c), z,
                p["w_br_fox"], p["w_br_mla"], p["w_br_dsa"], p["w_out"])
    x2 = _ffn(x2, p["norm_ffn"], p["w_ffn_gate"], p["w_ffn_up"], p["w_ffn_down"])
    return x2.reshape(b, t_len, d), new_state


def kernel(x_prompt, x_sample, cache_fox_k, cache_fox_v, cache_fox_logf, cache_mla_ckv, cache_mla_kpe, cache_dsa_k, cache_dsa_v, cache_dsa_idxk, norm_attn, w_in, b_forget, norm_mla_q, norm_mla_kv, w_mla_uq, w_mla_ukv, w_br_fox, w_br_mla, w_br_dsa, w_out, norm_ffn, w_ffn_gate, w_ffn_up, w_ffn_down, norm_final):
    depth = w_in.shape[0]
    past_len = cache_fox_k.shape[2]
    w_in_p, w_uq_p, w_uk_p, w_uv_p = _pack_weights(w_in, w_mla_uq, w_mla_ukv)
    bf = lambda a: a.astype(BF16)
    w_br_fox, w_br_mla, w_br_dsa, w_out = bf(w_br_fox), bf(w_br_mla), bf(w_br_dsa), bf(w_out)
    w_ffn_gate, w_ffn_up, w_ffn_down = bf(w_ffn_gate), bf(w_ffn_up), bf(w_ffn_down)
    caches = (cache_fox_k, cache_fox_v, cache_fox_logf, cache_mla_ckv, cache_mla_kpe,
              cache_dsa_k, cache_dsa_v, cache_dsa_idxk)

    tabs_p = _rope_tables(jnp.arange(x_prompt.shape[1]))
    tabs_s = _rope_tables(past_len + jnp.arange(x_sample.shape[1]))

    xp, xs = x_prompt, x_sample
    states_p, states_s = [], []
    for l in range(depth):
        p = dict(norm_attn=norm_attn[l], w_in=w_in_p[l], b_forget=b_forget[l], norm_mla_q=norm_mla_q[l],
                 norm_mla_kv=norm_mla_kv[l], w_uq=w_uq_p[l], w_uk=w_uk_p[l], w_uv=w_uv_p[l], w_br_fox=w_br_fox[l],
                 w_br_mla=w_br_mla[l], w_br_dsa=w_br_dsa[l], w_out=w_out[l], norm_ffn=norm_ffn[l],
                 w_ffn_gate=w_ffn_gate[l], w_ffn_up=w_ffn_up[l], w_ffn_down=w_ffn_down[l])
        xp, st_p = _layer(xp, None, tabs_p, p)
        xs, st_s = _layer(xs, tuple(c[l] for c in caches), tabs_s, p)
        states_p.append(st_p)
        states_s.append(st_s)

    stacked_p = [jnp.stack(a, axis=0) for a in zip(*states_p)]
    stacked_s = [jnp.stack(a, axis=0) for a in zip(*states_s)]
    yp = _rmsnorm(xp.reshape(-1, xp.shape[-1]), norm_final).reshape(xp.shape)
    ys = _rmsnorm(xs.reshape(-1, xs.shape[-1]), norm_final).reshape(xs.shape)
    out = [yp, ys]
    for sp, ss in zip(stacked_p, stacked_s):
        out += [sp, ss]
    return tuple(out)
```

```python
import functools
import math

import numpy as np
import jax
import jax.numpy as jnp
from jax import lax
from jax.experimental import pallas as pl
from jax.experimental.pallas import tpu as pltpu

F32 = jnp.float32
BF16 = jnp.bfloat16

CHUNK = 64
EPS = 1e-6
HEADS = 8
HEAD_DIM = 64
MLA_ROPE = 32
MLA_THETA = 10000.0
DSA_KV_HEADS = 2
ROPE_THETA = 500000.0
PART_ROT = HEAD_DIM // 4
TOPK_MAX = 256
BRANCH_W = HEADS * HEAD_DIM

LANES = 128
SUBLANES = 8
VMEM_LIMIT = 56 * 1024 * 1024
NEG = -1e30
LOG2E = math.log2(math.e)
HS = LANES
QW = HEADS * HS
VR = HEAD_DIM + 16

C_GZ = 0
C_FQ = 3072
C_FK = 4096
C_DQ = 5120
C_IQ = 6144
C_FV = 7168
C_CQ = 7680
C_CKV = 7936
C_DK = 8064
C_DV = 8192
C_MISC = 8320
C_FKC = 8448
N_PACK = 8960
M_IK, M_KR, M_FF, M_IW = 0, 64, 96, 104

KEY_NEG_INF = int(np.array(0xFF800000 ^ 0x7FFFFFFF, dtype=np.uint32).view(np.int32))
KEY_POS_INF = 0x7F800000
INT_MAX = 2 ** 31 - 1
HALF = 2 ** 15
PACK16 = 16

NT_DIMS = (((1,), (1,)), ((), ()))


def _cparams(*sem):
    return pltpu.CompilerParams(dimension_semantics=sem, vmem_limit_bytes=VMEM_LIMIT)


def _round_up(a, b):
    return -(-a // b) * b


def _row_tile(m, cap=512):
    t = cap
    while t > 8 and m % t:
        t //= 2
    assert m % t == 0
    return t


def _rms_mm_kernel(x_ref, g_ref, w_ref, o_ref, xn_ref, *, normalize):
    @pl.when(pl.program_id(1) == 0)
    def _():
        x = x_ref[...].astype(F32)
        if normalize:
            x = x * lax.rsqrt(jnp.mean(x * x, axis=-1, keepdims=True) + EPS) * g_ref[...]
        xn_ref[...] = x.astype(BF16)

    o_ref[...] = jnp.dot(xn_ref[...], w_ref[...], preferred_element_type=F32).astype(o_ref.dtype)


def _rms_matmul(x, col_block, k, g, w, *, tn, out_dtype=F32):
    m = x.shape[0]
    n = w.shape[1]
    tm = _row_tile(m)
    normalize = g is not None
    if g is None:
        g = jnp.ones((k,), F32)
    return pl.pallas_call(
        functools.partial(_rms_mm_kernel, normalize=normalize),
        grid=(m // tm, n // tn),
        in_specs=[pl.BlockSpec((tm, k), lambda i, j: (i, col_block)),
                  pl.BlockSpec((1, k), lambda i, j: (0, 0)),
                  pl.BlockSpec((k, tn), lambda i, j: (0, j))],
        out_specs=pl.BlockSpec((tm, tn), lambda i, j: (i, j)),
        out_shape=jax.ShapeDtypeStruct((m, n), out_dtype),
        scratch_shapes=[pltpu.VMEM((tm, k), BF16)],
        compiler_params=_cparams("parallel", "arbitrary"),
        name="rms_matmul",
    )(x, g.reshape(1, k).astype(F32), w)


N_TABS = 12
TAB_IQ, TAB_DK, TAB_DQ_HI, TAB_MLA = 0, 3, 6, 9


def _rope_tables(pos):
    lane = np.arange(LANES)

    def pattern(theta, rot, period, lane0, width):
        half = rot // 2
        inv_freq = theta ** (-jnp.arange(half, dtype=F32) * (2.0 / rot))
        ang = pos.astype(F32)[:, None] * inv_freq[None, :]
        cos, sin = jnp.cos(ang), jnp.sin(ang)
        inside = (lane >= lane0) & (lane < lane0 + width)
        r = (lane - lane0) % period
        first = inside & (r < half)
        second = inside & (r >= half) & (r < rot)
        fidx = np.where(first, r, np.where(second, r - half, 0))
        c = jnp.where((first | second)[None, :], cos[:, fidx], 1.0)
        s1 = jnp.where(first[None, :], -sin[:, fidx], 0.0)
        s2 = jnp.where(second[None, :], sin[:, fidx], 0.0)
        return [c, s1, s2]

    tabs = (pattern(ROPE_THETA, PART_ROT, HS, 0, HS)
            + pattern(ROPE_THETA, PART_ROT, HEAD_DIM, 0, LANES)
            + pattern(ROPE_THETA, PART_ROT, HEAD_DIM, HEAD_DIM, HEAD_DIM)
            + pattern(MLA_THETA, MLA_ROPE, MLA_ROPE, M_KR, MLA_ROPE))
    return jnp.stack(tabs, axis=0)


def _rot(x, tab_ref, t0, half):
    return (x * tab_ref[t0] + pltpu.roll(x, LANES - half, 1) * tab_ref[t0 + 1]
            + pltpu.roll(x, half, 1) * tab_ref[t0 + 2])


def _post_kernel(fq_ref, dq_ref, iq_ref, ckv_ref, dk_ref, misc_ref, qm_ref, tab_ref, bias_ref, gkv_ref,
                 fqa_o, dqa_o, iqa_o, mqa_o, ckv_o, dk_o, misc_o):
    half = PART_ROT // 2
    kv_rep = HEADS // DSA_KV_HEADS
    sc_dot = (HEAD_DIM ** -0.5) * LOG2E
    sc_mla = ((HEAD_DIM + MLA_ROPE) ** -0.5) * LOG2E
    for h in range(HEADS):
        sl = slice(h * HS, (h + 1) * HS)
        fqa_o[:, sl] = (fq_ref[:, sl] * sc_dot).astype(BF16)
        t_dq = TAB_IQ if h < kv_rep else TAB_DQ_HI
        dqa_o[:, sl] = (_rot(dq_ref[:, sl], tab_ref, t_dq, half) * sc_dot).astype(BF16)
        iqa_o[:, sl] = _rot(iq_ref[:, sl], tab_ref, TAB_IQ, half).astype(BF16)
        mqa_o[:, sl] = (_rot(qm_ref[:, sl], tab_ref, TAB_MLA, MLA_ROPE // 2) * sc_mla).astype(BF16)
    dk_o[...] = _rot(dk_ref[...], tab_ref, TAB_DK, half)

    ckv = ckv_ref[...]
    ckv_o[...] = ckv * lax.rsqrt(jnp.mean(ckv * ckv, axis=-1, keepdims=True) + EPS) * gkv_ref[...]

    x = misc_ref[...]
    roped = (x * (tab_ref[TAB_IQ] * tab_ref[TAB_MLA])
             + pltpu.roll(x, LANES - half, 1) * tab_ref[TAB_IQ + 1] + pltpu.roll(x, half, 1) * tab_ref[TAB_IQ + 2]
             + pltpu.roll(x, LANES - MLA_ROPE // 2, 1) * tab_ref[TAB_MLA + 1]
             + pltpu.roll(x, MLA_ROPE // 2, 1) * tab_ref[TAB_MLA + 2])
    lane = lax.broadcasted_iota(jnp.int32, x.shape, 1)
    logf = jax.nn.log_sigmoid(x + bias_ref[...])
    is_ff = jnp.where(lane >= M_FF, jnp.where(lane < M_IW, 1, 0), 0)
    misc_o[...] = jnp.where(is_ff == 1, logf, roped)


def _post(z, qm, tabs, b_forget, g_kv, t_len):
    m = z.shape[0]
    tm = _row_tile(m)
    if t_len % tm == 0:
        nt = t_len // tm
        tab_map = lambda i: (0, i % nt, 0)
    else:
        assert tm % t_len == 0
        tabs = jnp.tile(tabs, (1, tm // t_len, 1))
        tab_map = lambda i: (0, 0, 0)
    bias = jnp.zeros((1, LANES), F32).at[0, M_FF:M_FF + HEADS].set(b_forget.astype(F32))
    blk = lambda w, c0: pl.BlockSpec((tm, w), lambda i: (i, c0 // w))
    outs = [(QW, BF16), (QW, BF16), (QW, BF16), (QW, BF16), (LANES, F32), (LANES, F32), (LANES, F32)]
    return pl.pallas_call(
        _post_kernel,
        grid=(m // tm,),
        in_specs=[blk(QW, C_FQ), blk(QW, C_DQ), blk(QW, C_IQ), blk(LANES, C_CKV), blk(LANES, C_DK),
                  blk(LANES, C_MISC), pl.BlockSpec((tm, QW), lambda i: (i, 0)),
                  pl.BlockSpec((N_TABS, tm, LANES), tab_map),
                  pl.BlockSpec((1, LANES), lambda i: (0, 0)),
                  pl.BlockSpec((1, LANES), lambda i: (0, 0))],
        out_specs=[pl.BlockSpec((tm, w), lambda i: (i, 0)) for w, _ in outs],
        out_shape=[jax.ShapeDtypeStruct((m, w), dt) for w, dt in outs],
        compiler_params=_cparams("parallel"),
        name="post",
    )(z, z, z, z, z, z, qm, tabs, bias, g_kv.reshape(1, LANES).astype(F32))


def _cumsum_kernel(x_ref, o_ref):
    nc = x_ref.shape[1]
    row = lax.broadcasted_iota(jnp.int32, (LANES, LANES), 0)
    col = lax.broadcasted_iota(jnp.int32, (LANES, LANES), 1)
    tri = jnp.where(row <= col, 1.0, 0.0).astype(F32)

    def body(c, carry):
        cs = jnp.dot(x_ref[0, c], tri, precision=lax.Precision.HIGHEST, preferred_element_type=F32) + carry
        o_ref[0, c] = cs
        return cs[:, LANES - 1:LANES]

    lax.fori_loop(0, nc, body, jnp.zeros((HEADS, 1), F32))


def _cumsum(logf):
    b, lp, h = logf.shape
    nc = lp // LANES
    x = logf.transpose(0, 2, 1).reshape(b, h, nc, LANES).transpose(0, 2, 1, 3)
    out = pl.pallas_call(
        _cumsum_kernel,
        grid=(b,),
        in_specs=[pl.BlockSpec((1, nc, h, LANES), lambda i: (i, 0, 0, 0))],
        out_specs=pl.BlockSpec((1, nc, h, LANES), lambda i: (i, 0, 0, 0)),
        out_shape=jax.ShapeDtypeStruct((b, nc, h, LANES), F32),
        compiler_params=_cparams("parallel"),
        name="cumsum",
    )(x)
    return out.transpose(0, 2, 1, 3).reshape(b, h, lp)


def _store_value_tiles(vt_o, v):
    vt = v.T
    tk = v.shape[0]
    for h in range(v.shape[1] // HEAD_DIM):
        vt_o[0, 0, h * VR:h * VR + HEAD_DIM, :] = vt[h * HEAD_DIM:(h + 1) * HEAD_DIM, :].astype(BF16)
        vt_o[0, 0, h * VR + HEAD_DIM:(h + 1) * VR, :] = jnp.ones((VR - HEAD_DIM, tk), BF16)


def _kside_kernel(k_ref, v_ref, ka_o, vt_o):
    ka_o[0] = k_ref[0].astype(BF16)
    _store_value_tiles(vt_o, v_ref[0])


def _kside(k, v, tk):
    (ka, kcb, kw), (va, vcb, vw) = k, v
    b, lp = ka.shape[0], ka.shape[1]
    nkt = lp // tk
    vr = vw // HEAD_DIM * VR
    return pl.pallas_call(
        _kside_kernel,
        grid=(b, nkt),
        in_specs=[pl.BlockSpec((1, tk, kw), lambda bb, j: (bb, j, kcb)),
                  pl.BlockSpec((1, tk, vw), lambda bb, j: (bb, j, vcb))],
        out_specs=[pl.BlockSpec((1, tk, kw), lambda bb, j: (bb, j, 0)),
                   pl.BlockSpec((1, 1, vr, tk), lambda bb, j: (bb, j, 0, 0))],
        out_shape=[jax.ShapeDtypeStruct((b, lp, kw), BF16), jax.ShapeDtypeStruct((b, nkt, vr, tk), BF16)],
        compiler_params=_cparams("parallel", "parallel"),
        name="kside",
    )(ka, va)


def _mla_kv_kernel(ckv_ref, kpe_ref, wk_ref, wv_ref, ka_o, vt_o):
    c = ckv_ref[0].astype(BF16)
    k = jnp.dot(c, wk_ref[...], preferred_element_type=F32)
    kpe = kpe_ref[0]
    lane = lax.broadcasted_iota(jnp.int32, kpe.shape, 1)
    kpe = jnp.where(lane >= M_KR, jnp.where(lane < M_KR + MLA_ROPE, kpe, 0.0), 0.0)
    for h in range(HEADS):
        sl = slice(h * HS, (h + 1) * HS)
        ka_o[0, :, sl] = (k[:, sl] + kpe).astype(BF16)
    _store_value_tiles(vt_o, jnp.dot(c, wv_ref[...], preferred_element_type=F32))


def _mla_kv(ckv_all, kpe_all, wk, wv, tk):
    b, lp, _ = ckv_all.shape
    nkt = lp // tk
    return pl.pallas_call(
        _mla_kv_kernel,
        grid=(b, nkt),
        in_specs=[pl.BlockSpec((1, tk, LANES), lambda bb, j: (bb, j, 0)),
                  pl.BlockSpec((1, tk, LANES), lambda bb, j: (bb, j, 0)),
                  pl.BlockSpec(wk.shape, lambda bb, j: (0, 0)),
                  pl.BlockSpec(wv.shape, lambda bb, j: (0, 0))],
        out_specs=[pl.BlockSpec((1, tk, QW), lambda bb, j: (bb, j, 0)),
                   pl.BlockSpec((1, 1, HEADS * VR, tk), lambda bb, j: (bb, j, 0, 0))],
        out_shape=[jax.ShapeDtypeStruct((b, lp, QW), BF16), jax.ShapeDtypeStruct((b, nkt, HEADS * VR, tk), BF16)],
        compiler_params=_cparams("parallel", "parallel"),
        name="mla_kv",
    )(ckv_all, kpe_all, wk, wv)


def _attend_tile(score_fn, vt_fn, m_scr, acc_scr, s_scr, tq):
    for h in range(HEADS):
        s_scr[:, h * tq:(h + 1) * tq] = score_fn(h)
    s_all = s_scr[...]
    m_prev = m_scr[...]
    m_new = jnp.maximum(m_prev, jnp.max(s_all, axis=0, keepdims=True))
    alpha = jnp.exp2(m_prev - m_new)
    m_scr[...] = m_new
    p_all = jnp.exp2((s_all - m_new).astype(BF16))
    for h in range(HEADS):
        rows = slice(h * VR, (h + 1) * VR)
        cols = slice(h * tq, (h + 1) * tq)
        acc_scr[rows, :] = alpha[:, cols] * acc_scr[rows, :] + jnp.dot(vt_fn(h), p_all[:, cols],
                                                                     preferred_element_type=F32)


def _softmax_init(m_scr, acc_scr):
    m_scr[...] = jnp.full(m_scr.shape, NEG, F32)
    acc_scr[...] = jnp.zeros(acc_scr.shape, F32)


def _softmax_finish(o_ref, acc_scr):
    outs = [acc_scr[h * VR:h * VR + HEAD_DIM, :] / acc_scr[h * VR + HEAD_DIM:h * VR + HEAD_DIM + 1, :]
            for h in range(HEADS)]
    o_ref[0] = jnp.concatenate(outs, axis=0).T


def _last_allowed(q_pos, per_frame):
    return q_pos if per_frame else (q_pos | (CHUNK - 1))


def _flash_kernel(*refs, tq, tk, off, kv_len, per_frame, has_bias):
    if has_bias:
        qa_ref, ka_ref, vt_ref, qb_ref, kb_ref, o_ref, m_scr, acc_scr, s_scr = refs
    else:
        qa_ref, ka_ref, vt_ref, o_ref, m_scr, acc_scr, s_scr = refs
    i, j = pl.program_id(1), pl.program_id(2)
    q0 = i * tq + off
    k0 = j * tk
    k_last = k0 + tk - 1

    @pl.when(j == 0)
    def _():
        _softmax_init(m_scr, acc_scr)

    def step(masked):
        if masked:
            kpos = k0 + lax.broadcasted_iota(jnp.int32, (tk, tq), 0)
            qlim = _last_allowed(q0 + lax.broadcasted_iota(jnp.int32, (tk, tq), 1), per_frame)
            mask_bias = jnp.where(kpos <= qlim, jnp.where(kpos < kv_len, 0.0, NEG), NEG)

        def score(h):
            sl = slice(h * HS, (h + 1) * HS)
            s_t = lax.dot_general(ka_ref[0, :, sl], qa_ref[0, :, sl], NT_DIMS, preferred_element_type=F32)
            if has_bias:
                s_t = s_t + (qb_ref[0, h:h + 1, :] - kb_ref[0, :, h:h + 1])
            return s_t + mask_bias if masked else s_t

        _attend_tile(score, lambda h: vt_ref[0, 0, h * VR:(h + 1) * VR, :], m_scr, acc_scr, s_scr, tq)

    clear = jnp.logical_and(k_last <= _last_allowed(q0, per_frame), k_last < kv_len)
    needed = k0 <= _last_allowed(q0 + tq - 1, per_frame)

    @pl.when(clear)
    def _():
        step(False)

    @pl.when(jnp.logical_and(needed, jnp.logical_not(clear)))
    def _():
        step(True)

    @pl.when(j == pl.num_programs(2) - 1)
    def _():
        _softmax_finish(o_ref, acc_scr)


def _flash(qa, ka, vt, *, t_len, kv_len, per_frame, bias=None):
    b, tp, _ = qa.shape
    lp = ka.shape[1]
    tk = vt.shape[3]
    tq = min(512, tp)
    off = kv_len - t_len
    nq, nk = tp // tq, lp // tk

    def kidx(i, j):
        last = _last_allowed(i * tq + off + tq - 1, per_frame) // tk
        return jnp.minimum(j, jnp.minimum(last, nk - 1))

    operands = [qa, ka, vt]
    in_specs = [pl.BlockSpec((1, tq, QW), lambda bb, i, j: (bb, i, 0)),
                pl.BlockSpec((1, tk, QW), lambda bb, i, j: (bb, kidx(i, j), 0)),
                pl.BlockSpec((1, 1, HEADS * VR, tk), lambda bb, i, j: (bb, kidx(i, j), 0, 0))]
    if bias is not None:
        operands += list(bias)
        in_specs += [pl.BlockSpec((1, HEADS, tq), lambda bb, i, j: (bb, 0, i)),
                     pl.BlockSpec((1, tk, HEADS), lambda bb, i, j: (bb, kidx(i, j), 0))]
    kern = functools.partial(_flash_kernel, tq=tq, tk=tk, off=off, kv_len=kv_len, per_frame=per_frame,
                             has_bias=bias is not None)
    return pl.pallas_call(
        kern,
        grid=(b, nq, nk),
        in_specs=in_specs,
        out_specs=pl.BlockSpec((1, tq, BRANCH_W), lambda bb, i, j: (bb, i, 0)),
        out_shape=jax.ShapeDtypeStruct((b, tp, BRANCH_W), F32),
        scratch_shapes=[pltpu.VMEM((1, HEADS * tq), F32), pltpu.VMEM((HEADS * VR, tq), F32),
                        pltpu.VMEM((tk, HEADS * tq), F32)],
        compiler_params=_cparams("parallel", "parallel", "arbitrary"),
        name="flash_fox" if per_frame else "flash_mla",
    )(*operands)


def _sortable(x):
    bits = pltpu.bitcast(x, jnp.int32)
    return jnp.where(bits < 0, bits ^ INT_MAX, bits)


def _dsa_kernel(dqa_ref, iqa_ref, iw_ref, kk_ref, vt_ref, ik_ref, o_ref, key_scr, half_scr, j_scr, m_scr, acc_scr, s_scr,
                *, tq, tk, off, kv_len, n_sel, idx_bits):
    i = pl.program_id(1)
    q0 = i * tq + off
    qlim = (q0 + lax.broadcasted_iota(jnp.int32, (1, tq), 1)) | (CHUNK - 1)
    k_end = jnp.minimum(((q0 + tq - 1) | (CHUNK - 1)) + 1, kv_len)
    nkt = (k_end + tk - 1) // tk
    kv_rep = HEADS // DSA_KV_HEADS

    def score_tile(j, carry):
        ks = pl.multiple_of(j * tk, tk)
        ik = ik_ref[0, pl.ds(ks, tk), :]
        sc = jnp.zeros((tk, tq), F32)
        for h in range(HEADS):
            s = lax.dot_general(ik, iqa_ref[0, :, h * HS:(h + 1) * HS], NT_DIMS, preferred_element_type=F32)
            sc = sc + jnp.maximum(s, 0.0) * iw_ref[0, h:h + 1, :]
        kpos = ks + lax.broadcasted_iota(jnp.int32, (tk, tq), 0)
        adm = jnp.where(kpos <= qlim, jnp.where(kpos < kv_len, 1, 0), 0)
        key = jnp.where(adm == 1, _sortable(sc), KEY_NEG_INF)
        key_scr[j] = key
        half_scr[j] = lax.shift_right_arithmetic(key, 16).astype(jnp.int16)
        return carry

    lax.fori_loop(0, nkt, score_tile, 0)

    def count16(cand):
        def body(j, acc):
            ind = jnp.where(half_scr[j] >= cand, jnp.int16(1), jnp.int16(0))
            for r in range(tk // PACK16):
                acc = acc + ind[r * PACK16:(r + 1) * PACK16, :]
            return acc

        acc = lax.fori_loop(0, nkt, body, jnp.zeros((PACK16, tq), jnp.int16))
        return jnp.sum(acc.astype(jnp.int32), axis=0, keepdims=True)

    def bisect16(target):
        def bit_step(s, ans):
            cand = ans + jnp.left_shift(jnp.int32(1), 15 - s)
            return jnp.where(count16(cand.astype(jnp.int16)) >= target, cand, ans)

        return lax.fori_loop(0, 16, bit_step, jnp.full((1, tq), -HALF, jnp.int32))

    thr_hi = bisect16(n_sel)
    above = jnp.where(thr_hi < HALF - 1, count16(jnp.minimum(thr_hi + 1, HALF - 1).astype(jnp.int16)), 0)

    def low_tile(j, carry):
        key = key_scr[j]
        low = jnp.bitwise_and(key, 2 * HALF - 1) - HALF
        in_bucket = lax.shift_right_arithmetic(key, 16) == thr_hi
        half_scr[j] = jnp.where(in_bucket, low, -HALF).astype(jnp.int16)
        return carry

    lax.fori_loop(0, nkt, low_tile, 0)
    thr = thr_hi * (2 * HALF) + (bisect16(n_sel - above) + HALF)

    def count(pred):
        def body(j, acc):
            ind = pred(key_scr[j], j)
            return acc + jnp.sum(ind.reshape(tk // SUBLANES, SUBLANES, tq), axis=0)

        acc = lax.fori_loop(0, nkt, body, jnp.zeros((SUBLANES, tq), jnp.int32))
        return jnp.sum(acc, axis=0, keepdims=True)

    c_gt = count(lambda kt, j: jnp.where(kt > thr, 1, 0))
    c_eq = count(lambda kt, j: jnp.where(kt == thr, 1, 0))
    need = n_sel - c_gt
    j_scr[...] = jnp.full((1, tq), INT_MAX, jnp.int32)

    @pl.when(jnp.max(c_eq - need) > 0)
    def _():
        def idx_step(s, jv):
            cand = jv + jnp.left_shift(jnp.int32(1), idx_bits - 1 - s)

            def pred(kt, j):
                idx = j * tk + lax.broadcasted_iota(jnp.int32, (tk, tq), 0)
                return jnp.where(kt == thr, jnp.where(idx < cand, 1, 0), 0)

            return jnp.where(count(pred) < need, cand, jv)

        j_scr[...] = lax.fori_loop(0, idx_bits, idx_step, jnp.zeros((1, tq), jnp.int32))

    short = thr <= KEY_NEG_INF
    lo = jnp.where(short, KEY_NEG_INF, thr)
    j_lim = jnp.where(short, -1, j_scr[...])

    _softmax_init(m_scr, acc_scr)

    def attend_tile(j, carry):
        ks = pl.multiple_of(j * tk, tk)
        kt = key_scr[j]
        idx = ks + lax.broadcasted_iota(jnp.int32, (tk, tq), 0)
        sel = jnp.where(kt > lo, 1, jnp.where(kt == lo, jnp.where(idx <= j_lim, 1, 0), 0))
        sel = jnp.where(kt < KEY_POS_INF, sel, 0)
        mask_bias = jnp.where(sel == 1, 0.0, NEG)
        kk = kk_ref[0, pl.ds(ks, tk), :]

        def score(h):
            return lax.dot_general(kk, dqa_ref[0, :, h * HS:(h + 1) * HS], NT_DIMS,
                                   preferred_element_type=F32) + mask_bias

        _attend_tile(score, lambda h: vt_ref[0, j, (h // kv_rep) * VR:(h // kv_rep + 1) * VR, :],
                     m_scr, acc_scr, s_scr, tq)
        return carry

    lax.fori_loop(0, nkt, attend_tile, 0)
    _softmax_finish(o_ref, acc_scr)


def _dsa(dqa, iqa, iw_t, kk, vt, ik, *, t_len, kv_len):
    b, tp, _ = dqa.shape
    lp = kk.shape[1]
    tk = vt.shape[3]
    tq = min(256, tp)
    n_sel = min(TOPK_MAX, kv_len // 4)
    assert tk >= n_sel and lp % tk == 0 and tp % tq == 0
    nkt_max = lp // tk
    kern = functools.partial(_dsa_kernel, tq=tq, tk=tk, off=kv_len - t_len, kv_len=kv_len, n_sel=n_sel,
                             idx_bits=max(1, int(lp - 1).bit_length()))
    whole = lambda a: pl.BlockSpec((1,) + a.shape[1:], lambda bb, i: (bb,) + (0,) * (a.ndim - 1),
                                   pipeline_mode=pl.Buffered(1))
    return pl.pallas_call(
        kern,
        grid=(b, tp // tq),
        in_specs=[pl.BlockSpec((1, tq, QW), lambda bb, i: (bb, i, 0)),
                  pl.BlockSpec((1, tq, QW), lambda bb, i: (bb, i, 0)),
                  pl.BlockSpec((1, HEADS, tq), lambda bb, i: (bb, 0, i)),
                  whole(kk), whole(vt), whole(ik)],
        out_specs=pl.BlockSpec((1, tq, BRANCH_W), lambda bb, i: (bb, i, 0)),
        out_shape=jax.ShapeDtypeStruct((b, tp, BRANCH_W), F32),
        scratch_shapes=[pltpu.VMEM((nkt_max, tk, tq), jnp.int32), pltpu.VMEM((nkt_max, tk, tq), jnp.int16),
                        pltpu.VMEM((1, tq), jnp.int32),
                        pltpu.VMEM((1, HEADS * tq), F32), pltpu.VMEM((HEADS * VR, tq), F32),
                        pltpu.VMEM((tk, HEADS * tq), F32)],
        compiler_params=_cparams("parallel", "arbitrary"),
        name="dsa",
    )(dqa, iqa, iw_t, kk, vt, ik)


def _merge_kernel(x_ref, oa_ref, ob_ref, oc_ref, gz_ref, wa_ref, wb_ref, wc_ref, wo_ref, o_ref):
    d = x_ref.shape[1]
    merged = jnp.zeros(x_ref.shape, F32)
    for n, (o_br, w_br) in enumerate(((oa_ref, wa_ref), (ob_ref, wb_ref), (oc_ref, wc_ref))):
        gate = jax.nn.sigmoid(gz_ref[:, n * d:(n + 1) * d])
        merged = merged + gate * jnp.dot(o_br[...].astype(BF16), w_br[...], preferred_element_type=F32)
    o_ref[...] = x_ref[...] + jnp.dot(merged.astype(BF16), wo_ref[...], preferred_element_type=F32)


def _merge(x, oa, ob, oc, z, wa, wb, wc, wo):
    m, d = x.shape
    tm = _row_tile(m)
    row = lambda w: pl.BlockSpec((tm, w), lambda i: (i, 0))
    full = lambda a: pl.BlockSpec(a.shape, lambda i: (0, 0))
    return pl.pallas_call(
        _merge_kernel,
        grid=(m // tm,),
        in_specs=[row(d), row(BRANCH_W), row(BRANCH_W), row(BRANCH_W), row(3 * d), full(wa), full(wb), full(wc), full(wo)],
        out_specs=row(d),
        out_shape=jax.ShapeDtypeStruct((m, d), F32),
        compiler_params=_cparams("parallel"),
        name="merge",
    )(x, oa, ob, oc, z, wa, wb, wc, wo)


def _ffn_kernel(x_ref, g_ref, wg_ref, wu_ref, wd_ref, o_ref, hn_scr, acc_scr):
    j = pl.program_id(1)

    @pl.when(j == 0)
    def _():
        x = x_ref[...]
        hn_scr[...] = (x * lax.rsqrt(jnp.mean(x * x, axis=-1, keepdims=True) + EPS) * g_ref[...]).astype(BF16)
        acc_scr[...] = jnp.zeros(acc_scr.shape, F32)

    hn = hn_scr[...]
    a = jnp.dot(hn, wg_ref[...], preferred_element_type=F32)
    u = jnp.dot(hn, wu_ref[...], preferred_element_type=F32)
    acc_scr[...] += jnp.dot((jax.nn.silu(a) * u).astype(BF16), wd_ref[...], preferred_element_type=F32)

    @pl.when(j == pl.num_programs(1) - 1)
    def _():
        o_ref[...] = x_ref[...] + acc_scr[...]


def _ffn(x, g, wg, wu, wd):
    m, d = x.shape
    dff = wg.shape[1]
    tm = _row_tile(m)
    tf = dff // 2 if (dff // 2) % LANES == 0 else dff
    return pl.pallas_call(
        _ffn_kernel,
        grid=(m // tm, dff // tf),
        in_specs=[pl.BlockSpec((tm, d), lambda i, j: (i, 0)),
                  pl.BlockSpec((1, d), lambda i, j: (0, 0)),
                  pl.BlockSpec((d, tf), lambda i, j: (0, j)),
                  pl.BlockSpec((d, tf), lambda i, j: (0, j)),
                  pl.BlockSpec((tf, d), lambda i, j: (j, 0))],
        out_specs=pl.BlockSpec((tm, d), lambda i, j: (i, 0)),
        out_shape=jax.ShapeDtypeStruct((m, d), F32),
        scratch_shapes=[pltpu.VMEM((tm, d), BF16), pltpu.VMEM((tm, d), F32)],
        compiler_params=_cparams("parallel", "arbitrary"),
        name="ffn",
    )(x, g.reshape(1, d).astype(F32), wg, wu, wd)


def _rmsnorm_kernel(x_ref, g_ref, o_ref):
    x = x_ref[...]
    o_ref[...] = x * lax.rsqrt(jnp.mean(x * x, axis=-1, keepdims=True) + EPS) * g_ref[...]


def _rmsnorm(x, g):
    m, d = x.shape
    tm = _row_tile(m)
    return pl.pallas_call(
        _rmsnorm_kernel,
        grid=(m // tm,),
        in_specs=[pl.BlockSpec((tm, d), lambda i: (i, 0)), pl.BlockSpec((1, d), lambda i: (0, 0))],
        out_specs=pl.BlockSpec((tm, d), lambda i: (i, 0)),
        out_shape=jax.ShapeDtypeStruct((m, d), F32),
        compiler_params=_cparams("parallel"),
        name="final_rmsnorm",
    )(x, g.reshape(1, d).astype(F32))


def _head_strided(w, lane_of_head=lambda h: 0):
    lead = w.shape[:-1]
    w = w.reshape(lead + (HEADS, HEAD_DIM))
    zeros = jnp.zeros(lead + (HS - HEAD_DIM,), w.dtype)
    parts = []
    for h in range(HEADS):
        o = lane_of_head(h)
        assert o in (0, HS - HEAD_DIM)
        parts += [w[..., h, :], zeros] if o == 0 else [zeros, w[..., h, :]]
    return jnp.concatenate(parts, axis=-1)


def _pack_weights(w_in, w_mla_uq, w_mla_ukv):
    sizes = (BRANCH_W, BRANCH_W, BRANCH_W, HEADS, 256, 128, MLA_ROPE, BRANCH_W, 128, 128, BRANCH_W, HEAD_DIM, HEADS)
    offs = np.concatenate([[0], np.cumsum(sizes)])
    (fq, fk, fv, ff, cq, ckv, kr, dq, dk, dv, iq, ik, iw) = [w_in[..., int(offs[n]):int(offs[n + 1])]
                                                            for n in range(len(sizes))]
    gz = w_in[..., int(offs[-1]):]
    pad = jnp.zeros(w_in.shape[:-1] + (LANES - (HEAD_DIM + MLA_ROPE + 2 * HEADS),), w_in.dtype)
    kv_rep = HEADS // DSA_KV_HEADS
    w_in_p = jnp.concatenate(
        [gz, _head_strided(fq), _head_strided(fk), _head_strided(dq, lambda h: (h // kv_rep) * HEAD_DIM),
         _head_strided(iq), fv, cq, ckv, dk, dv, ik, kr, ff, iw, pad, fk], axis=-1).astype(BF16)
    assert w_in_p.shape[-1] == N_PACK

    depth = w_in.shape[0]
    uq = w_mla_uq.reshape(depth, w_mla_uq.shape[1], HEADS, HEAD_DIM + MLA_ROPE)
    uq = jnp.concatenate([uq, jnp.zeros(uq.shape[:-1] + (HS - HEAD_DIM - MLA_ROPE,), uq.dtype)], axis=-1)
    w_uq_p = uq.reshape(depth, -1, QW).astype(BF16)
    ukv = w_mla_ukv.reshape(depth, w_mla_ukv.shape[1], HEADS, 2 * HEAD_DIM)
    w_uk_p = _head_strided(ukv[..., :HEAD_DIM].reshape(depth, -1, BRANCH_W)).astype(BF16)
    w_uv_p = ukv[..., HEAD_DIM:].reshape(depth, -1, BRANCH_W).astype(BF16)
    return w_in_p, w_uq_p, w_uk_p, w_uv_p


def _with_past(past, new, lp):
    parts = [new] if past is None else [past.astype(new.dtype), new]
    rows = sum(p.shape[1] for p in parts)
    if lp > rows:
        parts.append(jnp.zeros((new.shape[0], lp - rows, new.shape[2]), new.dtype))
    return parts[0] if len(parts) == 1 else jnp.concatenate(parts, axis=1)


def _pad_axis(a, axis, size):
    if a.shape[axis] == size:
        return a
    widths = [(0, 0)] * a.ndim
    widths[axis] = (0, size - a.shape[axis])
    return jnp.pad(a, widths)


def _layer(x, past, tabs, p):
    b, t_len, d = x.shape
    m = b * t_len
    p_len = 0 if past is None else past[0].shape[1]
    kv_len = p_len + t_len
    tk = min(512, _round_up(kv_len, LANES))
    lp = _round_up(kv_len, tk)
    tp = _round_up(t_len, LANES)
    x2 = x.reshape(m, d)

    z = _rms_matmul(x2, 0, d, p["norm_attn"], p["w_in"], tn=N_PACK // 7)
    qm = _rms_matmul(z, C_CQ // 256, 256, p["norm_mla_q"], p["w_uq"], tn=256)
    fqa, dqa, iqa, mqa, ckv_n, dk_r, misc = _post(z, qm, tabs, p["b_forget"], p["norm_mla_kv"], t_len)

    z3 = z.reshape(b, t_len, N_PACK)
    r3 = lambda a: a.reshape(b, t_len, a.shape[-1])
    fk = z3[..., C_FKC:C_FKC + BRANCH_W].reshape(b, t_len, HEADS, HEAD_DIM)
    fv = z3[..., C_FV:C_FV + BRANCH_W]
    dv = z3[..., C_DV:C_DV + LANES]
    misc3, ckv3, dk3 = r3(misc), r3(ckv_n), r3(dk_r)
    logf = misc3[..., M_FF:M_FF + HEADS]
    new_state = (fk, fv.reshape(b, t_len, HEADS, HEAD_DIM), logf, ckv3, misc3[..., M_KR:M_KR + MLA_ROPE],
                 dk3.reshape(b, t_len, DSA_KV_HEADS, HEAD_DIM), dv.reshape(b, t_len, DSA_KV_HEADS, HEAD_DIM),
                 misc3[..., M_IK:M_IK + HEAD_DIM])

    if past is None:
        fox_k, fox_v = (z3, C_FK // QW, QW), (z3, C_FV // BRANCH_W, BRANCH_W)
        dsa_k, dsa_v = (dk3, 0, LANES), (z3, C_DV // LANES, LANES)
        logf_all, ckv_all, kpe_all = logf, ckv3, misc3
        ik_all = misc3.astype(BF16)
    else:
        (c_fk, c_fv, c_logf, c_ckv, c_kpe, c_dk, c_dv, c_ik) = past
        flat = lambda a: a.reshape(b, p_len, -1)
        c_fk_s = _pad_axis(c_fk, 3, HS).reshape(b, p_len, QW)
        fox_k = (_with_past(c_fk_s, z3[..., C_FK:C_FK + QW], lp), 0, QW)
        fox_v = (_with_past(flat(c_fv), fv, lp), 0, BRANCH_W)
        dsa_k = (_with_past(flat(c_dk), dk3, lp), 0, LANES)
        dsa_v = (_with_past(flat(c_dv), dv, lp), 0, LANES)
        logf_all = _with_past(c_logf, logf, lp)
        ckv_all = _with_past(c_ckv, ckv3, lp)
        c_kpe_s = jnp.pad(c_kpe, ((0, 0), (0, 0), (M_KR, LANES - M_KR - MLA_ROPE)))
        kpe_all = _with_past(c_kpe_s, misc3, lp)
        ik_all = _with_past(_pad_axis(c_ik, 2, LANES), misc3, lp).astype(BF16)

    pad_q = lambda a: _pad_axis(a.reshape(b, t_len, a.shape[-1]), 1, tp)

    cum = _cumsum(_pad_axis(logf_all, 1, lp)) * LOG2E
    q_bias = _pad_axis(cum[:, :, kv_len - t_len:kv_len], 2, tp)
    fka, fvt = _kside(fox_k, fox_v, tk)
    o_a = _flash(pad_q(fqa), fka, fvt, t_len=t_len, kv_len=kv_len, per_frame=True,
                 bias=(q_bias, cum.transpose(0, 2, 1)))

    mka, mvt = _mla_kv(_pad_axis(ckv_all, 1, lp), _pad_axis(kpe_all, 1, lp), p["w_uk"], p["w_uv"], tk)
    o_b = _flash(pad_q(mqa), mka, mvt, t_len=t_len, kv_len=kv_len, per_frame=False)

    dka, dvt = _kside(dsa_k, dsa_v, tk)
    iw_t = _pad_axis(misc3[..., M_IW:M_IW + HEADS].transpose(0, 2, 1), 2, tp)
    o_c = _dsa(pad_q(dqa), pad_q(iqa), iw_t, dka, dvt, _pad_axis(ik_all, 1, lp), t_len=t_len, kv_len=kv_len)

    unpad = lambda o: o[:, :t_len].reshape(m, BRANCH_W)
    x2 = _merge(x2, unpad(o_a), unpad(o_b), unpad(o_c), z,
                p["w_br_fox"], p["w_br_mla"], p["w_br_dsa"], p["w_out"])
    x2 = _ffn(x2, p["norm_ffn"], p["w_ffn_gate"], p["w_ffn_up"], p["w_ffn_down"])
    return x2.reshape(b, t_len, d), new_state


def kernel(x_prompt, x_sample, cache_fox_k, cache_fox_v, cache_fox_logf, cache_mla_ckv, cache_mla_kpe, cache_dsa_k, cache_dsa_v, cache_dsa_idxk, norm_attn, w_in, b_forget, norm_mla_q, norm_mla_kv, w_mla_uq, w_mla_ukv, w_br_fox, w_br_mla, w_br_dsa, w_out, norm_ffn, w_ffn_gate, w_ffn_up, w_ffn_down, norm_final):
    depth = w_in.shape[0]
    past_len = cache_fox_k.shape[2]
    w_in_p, w_uq_p, w_uk_p, w_uv_p = _pack_weights(w_in, w_mla_uq, w_mla_ukv)
    bf = lambda a: a.astype(BF16)
    w_br_fox, w_br_mla, w_br_dsa, w_out = bf(w_br_fox), bf(w_br_mla), bf(w_br_dsa), bf(w_out)
    w_ffn_gate, w_ffn_up, w_ffn_down = bf(w_ffn_gate), bf(w_ffn_up), bf(w_ffn_down)
    caches = (cache_fox_k, cache_fox_v, cache_fox_logf, cache_mla_ckv, cache_mla_kpe,
              cache_dsa_k, cache_dsa_v, cache_dsa_idxk)

    tabs_p = _rope_tables(jnp.arange(x_prompt.shape[1]))
    tabs_s = _rope_tables(past_len + jnp.arange(x_sample.shape[1]))

    xp, xs = x_prompt, x_sample
    states_p, states_s = [], []
    for l in range(depth):
        p = dict(norm_attn=norm_attn[l], w_in=w_in_p[l], b_forget=b_forget[l], norm_mla_q=norm_mla_q[l],
                 norm_mla_kv=norm_mla_kv[l], w_uq=w_uq_p[l], w_uk=w_uk_p[l], w_uv=w_uv_p[l], w_br_fox=w_br_fox[l],
                 w_br_mla=w_br_mla[l], w_br_dsa=w_br_dsa[l], w_out=w_out[l], norm_ffn=norm_ffn[l],
                 w_ffn_gate=w_ffn_gate[l], w_ffn_up=w_ffn_up[l], w_ffn_down=w_ffn_down[l])
        xp, st_p = _layer(xp, None, tabs_p, p)
        xs, st_s = _layer(xs, tuple(c[l] for c in caches), tabs_s, p)
        states_p.append(st_p)
        states_s.append(st_s)

    stacked_p = [jnp.stack(a, axis=0) for a in zip(*states_p)]
    stacked_s = [jnp.stack(a, axis=0) for a in zip(*states_s)]
    yp = _rmsnorm(xp.reshape(-1, xp.shape[-1]), norm_final).reshape(xp.shape)
    ys = _rmsnorm(xs.reshape(-1, xs.shape[-1]), norm_final).reshape(xs.shape)
    out = [yp, ys]
    for sp, ss in zip(stacked_p, stacked_s):
        out += [sp, ss]
    return tuple(out)
```

```python
import functools
import math

import numpy as np
import jax
import jax.numpy as jnp
from jax import lax
from jax.experimental import pallas as pl
from jax.experimental.pallas import tpu as pltpu

F32 = jnp.float32
BF16 = jnp.bfloat16

CHUNK = 64
EPS = 1e-6
HEADS = 8
HEAD_DIM = 64
MLA_ROPE = 32
MLA_THETA = 10000.0
DSA_KV_HEADS = 2
ROPE_THETA = 500000.0
PART_ROT = HEAD_DIM // 4
TOPK_MAX = 256
BRANCH_W = HEADS * HEAD_DIM

LANES = 128
SUBLANES = 8
VMEM_LIMIT = 56 * 1024 * 1024
NEG = -1e30
LOG2E = math.log2(math.e)
HS = LANES
QW = HEADS * HS
VR = HEAD_DIM + 16

C_GZ = 0
C_FQ = 3072
C_FK = 4096
C_DQ = 5120
C_IQ = 6144
C_FV = 7168
C_FKC = 7680
C_CQ = 8192
C_CKV = 8448
C_DK = 8576
C_DV = 8704
C_MISC = 8832
N_PACK = 8960
M_IK, M_KR, M_FF, M_IW = 0, 64, 96, 104

KEY_NEG_INF = int(np.array(0xFF800000 ^ 0x7FFFFFFF, dtype=np.uint32).view(np.int32))
KEY_POS_INF = 0x7F800000
INT_MAX = 2 ** 31 - 1
HALF = 2 ** 15
PACK16 = 16

NT_DIMS = (((1,), (1,)), ((), ()))


def _cparams(*sem):
    return pltpu.CompilerParams(dimension_semantics=sem, vmem_limit_bytes=VMEM_LIMIT)


def _round_up(a, b):
    return -(-a // b) * b


def _row_tile(m, cap=512):
    t = cap
    while t > 8 and m % t:
        t //= 2
    assert m % t == 0
    return t


def _rms_mm_kernel(x_ref, g_ref, w_ref, o_ref, xn_ref, *, normalize):
    @pl.when(pl.program_id(1) == 0)
    def _():
        x = x_ref[...].astype(F32)
        if normalize:
            x = x * lax.rsqrt(jnp.mean(x * x, axis=-1, keepdims=True) + EPS) * g_ref[...]
        xn_ref[...] = x.astype(BF16)

    o_ref[...] = jnp.dot(xn_ref[...], w_ref[...], preferred_element_type=F32).astype(o_ref.dtype)


def _rms_matmul(x, col_block, k, g, w, *, tn, out_dtype=F32):
    m = x.shape[0]
    n = w.shape[1]
    tm = _row_tile(m)
    normalize = g is not None
    if g is None:
        g = jnp.ones((k,), F32)
    return pl.pallas_call(
        functools.partial(_rms_mm_kernel, normalize=normalize),
        grid=(m // tm, n // tn),
        in_specs=[pl.BlockSpec((tm, k), lambda i, j: (i, col_block)),
                  pl.BlockSpec((1, k), lambda i, j: (0, 0)),
                  pl.BlockSpec((k, tn), lambda i, j: (0, j))],
        out_specs=pl.BlockSpec((tm, tn), lambda i, j: (i, j)),
        out_shape=jax.ShapeDtypeStruct((m, n), out_dtype),
        scratch_shapes=[pltpu.VMEM((tm, k), BF16)],
        compiler_params=_cparams("parallel", "arbitrary"),
        name="rms_matmul",
    )(x, g.reshape(1, k).astype(F32), w)


N_TABS = 12
TAB_IQ, TAB_DK, TAB_DQ_HI, TAB_MLA = 0, 3, 6, 9


def _rope_tables(pos):
    lane = np.arange(LANES)

    def pattern(theta, rot, period, lane0, width):
        half = rot // 2
        inv_freq = theta ** (-jnp.arange(half, dtype=F32) * (2.0 / rot))
        ang = pos.astype(F32)[:, None] * inv_freq[None, :]
        cos, sin = jnp.cos(ang), jnp.sin(ang)
        inside = (lane >= lane0) & (lane < lane0 + width)
        r = (lane - lane0) % period
        first = inside & (r < half)
        second = inside & (r >= half) & (r < rot)
        fidx = np.where(first, r, np.where(second, r - half, 0))
        c = jnp.where((first | second)[None, :], cos[:, fidx], 1.0)
        s1 = jnp.where(first[None, :], -sin[:, fidx], 0.0)
        s2 = jnp.where(second[None, :], sin[:, fidx], 0.0)
        return [c, s1, s2]

    tabs = (pattern(ROPE_THETA, PART_ROT, HS, 0, HS)
            + pattern(ROPE_THETA, PART_ROT, HEAD_DIM, 0, LANES)
            + pattern(ROPE_THETA, PART_ROT, HEAD_DIM, HEAD_DIM, HEAD_DIM)
            + pattern(MLA_THETA, MLA_ROPE, MLA_ROPE, M_KR, MLA_ROPE))
    return jnp.stack(tabs, axis=0)


def _rot(x, tab_ref, t0, half):
    return (x * tab_ref[t0] + pltpu.roll(x, LANES - half, 1) * tab_ref[t0 + 1]
            + pltpu.roll(x, half, 1) * tab_ref[t0 + 2])


def _post_kernel(fq_ref, dq_ref, iq_ref, ckv_ref, dk_ref, misc_ref, qm_ref, tab_ref, bias_ref, gkv_ref,
                 fqa_o, dqa_o, iqa_o, mqa_o, ckv_o, dk_o, misc_o):
    half = PART_ROT // 2
    kv_rep = HEADS // DSA_KV_HEADS
    sc_dot = (HEAD_DIM ** -0.5) * LOG2E
    sc_mla = ((HEAD_DIM + MLA_ROPE) ** -0.5) * LOG2E
    for h in range(HEADS):
        sl = slice(h * HS, (h + 1) * HS)
        fqa_o[:, sl] = (fq_ref[:, sl] * sc_dot).astype(BF16)
        t_dq = TAB_IQ if h < kv_rep else TAB_DQ_HI
        dqa_o[:, sl] = (_rot(dq_ref[:, sl], tab_ref, t_dq, half) * sc_dot).astype(BF16)
        iqa_o[:, sl] = _rot(iq_ref[:, sl], tab_ref, TAB_IQ, half).astype(BF16)
        mqa_o[:, sl] = (_rot(qm_ref[:, sl], tab_ref, TAB_MLA, MLA_ROPE // 2) * sc_mla).astype(BF16)
    dk_o[...] = _rot(dk_ref[...], tab_ref, TAB_DK, half)

    ckv = ckv_ref[...]
    ckv_o[...] = ckv * lax.rsqrt(jnp.mean(ckv * ckv, axis=-1, keepdims=True) + EPS) * gkv_ref[...]

    x = misc_ref[...]
    roped = (x * (tab_ref[TAB_IQ] * tab_ref[TAB_MLA])
             + pltpu.roll(x, LANES - half, 1) * tab_ref[TAB_IQ + 1] + pltpu.roll(x, half, 1) * tab_ref[TAB_IQ + 2]
             + pltpu.roll(x, LANES - MLA_ROPE // 2, 1) * tab_ref[TAB_MLA + 1]
             + pltpu.roll(x, MLA_ROPE // 2, 1) * tab_ref[TAB_MLA + 2])
    lane = lax.broadcasted_iota(jnp.int32, x.shape, 1)
    logf = jax.nn.log_sigmoid(x + bias_ref[...])
    is_ff = jnp.where(lane >= M_FF, jnp.where(lane < M_IW, 1, 0), 0)
    misc_o[...] = jnp.where(is_ff == 1, logf, roped)


def _post(z, qm, tabs, b_forget, g_kv, t_len):
    m = z.shape[0]
    tm = _row_tile(m)
    if t_len % tm == 0:
        nt = t_len // tm
        tab_map = lambda i: (0, i % nt, 0)
    else:
        assert tm % t_len == 0
        tabs = jnp.tile(tabs, (1, tm // t_len, 1))
        tab_map = lambda i: (0, 0, 0)
    bias = jnp.zeros((1, LANES), F32).at[0, M_FF:M_FF + HEADS].set(b_forget.astype(F32))
    blk = lambda w, c0: pl.BlockSpec((tm, w), lambda i: (i, c0 // w))
    outs = [(QW, BF16), (QW, BF16), (QW, BF16), (QW, BF16), (LANES, F32), (LANES, F32), (LANES, F32)]
    return pl.pallas_call(
        _post_kernel,
        grid=(m // tm,),
        in_specs=[blk(QW, C_FQ), blk(QW, C_DQ), blk(QW, C_IQ), blk(LANES, C_CKV), blk(LANES, C_DK),
                  blk(LANES, C_MISC), pl.BlockSpec((tm, QW), lambda i: (i, 0)),
                  pl.BlockSpec((N_TABS, tm, LANES), tab_map),
                  pl.BlockSpec((1, LANES), lambda i: (0, 0)),
                  pl.BlockSpec((1, LANES), lambda i: (0, 0))],
        out_specs=[pl.BlockSpec((tm, w), lambda i: (i, 0)) for w, _ in outs],
        out_shape=[jax.ShapeDtypeStruct((m, w), dt) for w, dt in outs],
        compiler_params=_cparams("parallel"),
        name="post",
    )(z, z, z, z, z, z, qm, tabs, bias, g_kv.reshape(1, LANES).astype(F32))


STATE_WIDTHS = (BRANCH_W, BRANCH_W, HEADS, LANES, MLA_ROPE, LANES, LANES, HEAD_DIM)


def _state_kernel(*refs):
    n = len(STATE_WIDTHS)
    fk_ref, fv_ref, dv_ref, ckv_ref, dk_ref, misc_ref = refs[:6]
    fk_o, fv_o, logf_o, ckv_o, kpe_o, dk_o, dv_o, ik_o = refs[len(refs) - n:]
    fk_o[0] = fk_ref[...]
    fv_o[0] = fv_ref[...]
    logf_o[0] = misc_ref[:, M_FF:M_FF + HEADS]
    ckv_o[0] = ckv_ref[...]
    kpe_o[0] = misc_ref[:, M_KR:M_KR + MLA_ROPE]
    dk_o[0] = dk_ref[...]
    dv_o[0] = dv_ref[...]
    ik_o[0] = misc_ref[:, M_IK:M_IK + HEAD_DIM]


def _write_state(layer, depth, bufs, z, ckv_n, dk_r, misc):
    m = z.shape[0]
    tm = _row_tile(m)
    n = len(STATE_WIDTHS)
    blk = lambda w, c0: pl.BlockSpec((tm, w), lambda i: (i, c0 // w))
    in_specs = [blk(BRANCH_W, C_FKC), blk(BRANCH_W, C_FV), blk(LANES, C_DV),
                pl.BlockSpec((tm, LANES), lambda i: (i, 0)), pl.BlockSpec((tm, LANES), lambda i: (i, 0)),
                pl.BlockSpec((tm, LANES), lambda i: (i, 0))]
    operands = [z, z, z, ckv_n, dk_r, misc]
    aliases = {}
    if bufs is not None:
        in_specs += [pl.BlockSpec(memory_space=pl.ANY)] * n
        aliases = {len(operands) + k: k for k in range(n)}
        operands += list(bufs)
    return pl.pallas_call(
        _state_kernel,
        grid=(m // tm,),
        in_specs=in_specs,
        out_specs=[pl.BlockSpec((1, tm, w), lambda i: (layer, i, 0)) for w in STATE_WIDTHS],
        out_shape=[jax.ShapeDtypeStruct((depth, m, w), F32) for w in STATE_WIDTHS],
        input_output_aliases=aliases,
        compiler_params=_cparams("arbitrary"),
        name="state",
    )(*operands)


def _cumsum_kernel(x_ref, o_ref):
    nc = x_ref.shape[1]
    row = lax.broadcasted_iota(jnp.int32, (LANES, LANES), 0)
    col = lax.broadcasted_iota(jnp.int32, (LANES, LANES), 1)
    tri = jnp.where(row <= col, 1.0, 0.0).astype(F32)

    def body(c, carry):
        cs = jnp.dot(x_ref[0, c], tri, precision=lax.Precision.HIGHEST, preferred_element_type=F32) + carry
        o_ref[0, c] = cs
        return cs[:, LANES - 1:LANES]

    lax.fori_loop(0, nc, body, jnp.zeros((HEADS, 1), F32))


def _cumsum(logf):
    b, lp, h = logf.shape
    nc = lp // LANES
    x = logf.transpose(0, 2, 1).reshape(b, h, nc, LANES).transpose(0, 2, 1, 3)
    out = pl.pallas_call(
        _cumsum_kernel,
        grid=(b,),
        in_specs=[pl.BlockSpec((1, nc, h, LANES), lambda i: (i, 0, 0, 0))],
        out_specs=pl.BlockSpec((1, nc, h, LANES), lambda i: (i, 0, 0, 0)),
        out_shape=jax.ShapeDtypeStruct((b, nc, h, LANES), F32),
        compiler_params=_cparams("parallel"),
        name="cumsum",
    )(x)
    return out.transpose(0, 2, 1, 3).reshape(b, h, lp)


def _store_value_tiles(vt_o, v):
    vt = v.T
    tk = v.shape[0]
    for h in range(v.shape[1] // HEAD_DIM):
        vt_o[0, 0, h * VR:h * VR + HEAD_DIM, :] = vt[h * HEAD_DIM:(h + 1) * HEAD_DIM, :].astype(BF16)
        vt_o[0, 0, h * VR + HEAD_DIM:(h + 1) * VR, :] = jnp.ones((VR - HEAD_DIM, tk), BF16)


def _kside_kernel(k_ref, v_ref, ka_o, vt_o):
    ka_o[0] = k_ref[0].astype(BF16)
    _store_value_tiles(vt_o, v_ref[0])


def _kside(k, v, tk):
    (ka, kcb, kw), (va, vcb, vw) = k, v
    b, lp = ka.shape[0], ka.shape[1]
    nkt = lp // tk
    vr = vw // HEAD_DIM * VR
    return pl.pallas_call(
        _kside_kernel,
        grid=(b, nkt),
        in_specs=[pl.BlockSpec((1, tk, kw), lambda bb, j: (bb, j, kcb)),
                  pl.BlockSpec((1, tk, vw), lambda bb, j: (bb, j, vcb))],
        out_specs=[pl.BlockSpec((1, tk, kw), lambda bb, j: (bb, j, 0)),
                   pl.BlockSpec((1, 1, vr, tk), lambda bb, j: (bb, j, 0, 0))],
        out_shape=[jax.ShapeDtypeStruct((b, lp, kw), BF16), jax.ShapeDtypeStruct((b, nkt, vr, tk), BF16)],
        compiler_params=_cparams("parallel", "parallel"),
        name="kside",
    )(ka, va)


def _mla_kv_kernel(ckv_ref, kpe_ref, wk_ref, wv_ref, ka_o, vt_o):
    c = ckv_ref[0].astype(BF16)
    k = jnp.dot(c, wk_ref[...], preferred_element_type=F32)
    kpe = kpe_ref[0]
    lane = lax.broadcasted_iota(jnp.int32, kpe.shape, 1)
    kpe = jnp.where(lane >= M_KR, jnp.where(lane < M_KR + MLA_ROPE, kpe, 0.0), 0.0)
    for h in range(HEADS):
        sl = slice(h * HS, (h + 1) * HS)
        ka_o[0, :, sl] = (k[:, sl] + kpe).astype(BF16)
    _store_value_tiles(vt_o, jnp.dot(c, wv_ref[...], preferred_element_type=F32))


def _mla_kv(ckv_all, kpe_all, wk, wv, tk):
    b, lp, _ = ckv_all.shape
    nkt = lp // tk
    return pl.pallas_call(
        _mla_kv_kernel,
        grid=(b, nkt),
        in_specs=[pl.BlockSpec((1, tk, LANES), lambda bb, j: (bb, j, 0)),
                  pl.BlockSpec((1, tk, LANES), lambda bb, j: (bb, j, 0)),
                  pl.BlockSpec(wk.shape, lambda bb, j: (0, 0)),
                  pl.BlockSpec(wv.shape, lambda bb, j: (0, 0))],
        out_specs=[pl.BlockSpec((1, tk, QW), lambda bb, j: (bb, j, 0)),
                   pl.BlockSpec((1, 1, HEADS * VR, tk), lambda bb, j: (bb, j, 0, 0))],
        out_shape=[jax.ShapeDtypeStruct((b, lp, QW), BF16), jax.ShapeDtypeStruct((b, nkt, HEADS * VR, tk), BF16)],
        compiler_params=_cparams("parallel", "parallel"),
        name="mla_kv",
    )(ckv_all, kpe_all, wk, wv)


def _attend_tile(score_fn, vt_fn, m_scr, acc_scr, s_scr, tq):
    for h in range(HEADS):
        s_scr[:, h * tq:(h + 1) * tq] = score_fn(h)
    s_all = s_scr[...]
    m_prev = m_scr[...]
    m_new = jnp.maximum(m_prev, jnp.max(s_all, axis=0, keepdims=True))
    alpha = jnp.exp2(m_prev - m_new)
    m_scr[...] = m_new
    p_all = jnp.exp2((s_all - m_new).astype(BF16))
    for h in range(HEADS):
        rows = slice(h * VR, (h + 1) * VR)
        cols = slice(h * tq, (h + 1) * tq)
        acc_scr[rows, :] = alpha[:, cols] * acc_scr[rows, :] + jnp.dot(vt_fn(h), p_all[:, cols],
                                                                     preferred_element_type=F32)


def _softmax_init(m_scr, acc_scr):
    m_scr[...] = jnp.full(m_scr.shape, NEG, F32)
    acc_scr[...] = jnp.zeros(acc_scr.shape, F32)


def _softmax_finish(o_ref, acc_scr):
    outs = [acc_scr[h * VR:h * VR + HEAD_DIM, :] / acc_scr[h * VR + HEAD_DIM:h * VR + HEAD_DIM + 1, :]
            for h in range(HEADS)]
    o_ref[0] = jnp.concatenate(outs, axis=0).T


def _last_allowed(q_pos, per_frame):
    return q_pos if per_frame else (q_pos | (CHUNK - 1))


def _flash_kernel(*refs, tq, tk, nk, off, kv_len, per_frame, has_bias):
    if has_bias:
        i_tab, j_tab, qa_ref, ka_ref, vt_ref, qb_ref, kb_ref, o_ref, m_scr, acc_scr, s_scr = refs
    else:
        i_tab, j_tab, qa_ref, ka_ref, vt_ref, o_ref, m_scr, acc_scr, s_scr = refs
    i, j = i_tab[pl.program_id(1)], j_tab[pl.program_id(1)]
    q0 = i * tq + off
    k0 = j * tk
    k_last = k0 + tk - 1

    @pl.when(j == 0)
    def _():
        _softmax_init(m_scr, acc_scr)

    def step(masked):
        if masked:
            kpos = k0 + lax.broadcasted_iota(jnp.int32, (tk, tq), 0)
            qlim = _last_allowed(q0 + lax.broadcasted_iota(jnp.int32, (tk, tq), 1), per_frame)
            mask_bias = jnp.where(kpos <= qlim, jnp.where(kpos < kv_len, 0.0, NEG), NEG)

        def score(h):
            sl = slice(h * HS, (h + 1) * HS)
            s_t = lax.dot_general(ka_ref[0, :, sl], qa_ref[0, :, sl], NT_DIMS, preferred_element_type=F32)
            if has_bias:
                s_t = s_t + (qb_ref[0, h:h + 1, :] - kb_ref[0, :, h:h + 1])
            return s_t + mask_bias if masked else s_t

        _attend_tile(score, lambda h: vt_ref[0, 0, h * VR:(h + 1) * VR, :], m_scr, acc_scr, s_scr, tq)

    clear = jnp.logical_and(k_last <= _last_allowed(q0, per_frame), k_last < kv_len)

    @pl.when(clear)
    def _():
        step(False)

    @pl.when(jnp.logical_not(clear))
    def _():
        step(True)

    @pl.when(j == jnp.minimum(_last_allowed(q0 + tq - 1, per_frame) // tk, nk - 1))
    def _():
        _softmax_finish(o_ref, acc_scr)


def _flash(qa, ka, vt, *, t_len, kv_len, per_frame, bias=None):
    b, tp, _ = qa.shape
    lp = ka.shape[1]
    tk = vt.shape[3]
    tq = min(512, tp)
    off = kv_len - t_len
    nq, nk = tp // tq, lp // tk

    pairs = [(i, j) for i in range(nq)
             for j in range(min(_last_allowed(i * tq + off + tq - 1, per_frame) // tk, nk - 1) + 1)]
    i_tab = jnp.asarray([p[0] for p in pairs], jnp.int32)
    j_tab = jnp.asarray([p[1] for p in pairs], jnp.int32)

    operands = [qa, ka, vt]
    in_specs = [pl.BlockSpec((1, tq, QW), lambda bb, p, it, jt: (bb, it[p], 0)),
                pl.BlockSpec((1, tk, QW), lambda bb, p, it, jt: (bb, jt[p], 0)),
                pl.BlockSpec((1, 1, HEADS * VR, tk), lambda bb, p, it, jt: (bb, jt[p], 0, 0))]
    if bias is not None:
        operands += list(bias)
        in_specs += [pl.BlockSpec((1, HEADS, tq), lambda bb, p, it, jt: (bb, 0, it[p])),
                     pl.BlockSpec((1, tk, HEADS), lambda bb, p, it, jt: (bb, jt[p], 0))]
    kern = functools.partial(_flash_kernel, tq=tq, tk=tk, nk=nk, off=off, kv_len=kv_len, per_frame=per_frame,
                             has_bias=bias is not None)
    return pl.pallas_call(
        kern,
        grid_spec=pltpu.PrefetchScalarGridSpec(
            num_scalar_prefetch=2,
            grid=(b, len(pairs)),
            in_specs=in_specs,
            out_specs=pl.BlockSpec((1, tq, BRANCH_W), lambda bb, p, it, jt: (bb, it[p], 0)),
            scratch_shapes=[pltpu.VMEM((1, HEADS * tq), F32), pltpu.VMEM((HEADS * VR, tq), F32),
                            pltpu.VMEM((tk, HEADS * tq), F32)]),
        out_shape=jax.ShapeDtypeStruct((b, tp, BRANCH_W), F32),
        compiler_params=_cparams("parallel", "arbitrary"),
        name="flash_fox" if per_frame else "flash_mla",
    )(i_tab, j_tab, *operands)


def _sortable(x):
    bits = pltpu.bitcast(x, jnp.int32)
    return jnp.where(bits < 0, bits ^ INT_MAX, bits)


def _dsa_kernel(dqa_ref, iqa_ref, iw_ref, kk_ref, vt_ref, ik_ref, o_ref, key_scr, half_scr, j_scr, m_scr, acc_scr, s_scr,
                *, tq, tk, off, kv_len, n_sel, idx_bits):
    i = pl.program_id(1)
    q0 = i * tq + off
    qlim = (q0 + lax.broadcasted_iota(jnp.int32, (1, tq), 1)) | (CHUNK - 1)
    k_end = jnp.minimum(((q0 + tq - 1) | (CHUNK - 1)) + 1, kv_len)
    nkt = (k_end + tk - 1) // tk
    kv_rep = HEADS // DSA_KV_HEADS

    def score_tile(j, carry):
        ks = pl.multiple_of(j * tk, tk)
        ik = ik_ref[0, pl.ds(ks, tk), :]
        sc = jnp.zeros((tk, tq), F32)
        for h in range(HEADS):
            s = lax.dot_general(ik, iqa_ref[0, :, h * HS:(h + 1) * HS], NT_DIMS, preferred_element_type=F32)
            sc = sc + jnp.maximum(s, 0.0) * iw_ref[0, h:h + 1, :]
        kpos = ks + lax.broadcasted_iota(jnp.int32, (tk, tq), 0)
        adm = jnp.where(kpos <= qlim, jnp.where(kpos < kv_len, 1, 0), 0)
        key = jnp.where(adm == 1, _sortable(sc), KEY_NEG_INF)
        key_scr[j] = key
        half_scr[j] = lax.shift_right_arithmetic(key, 16).astype(jnp.int16)
        return carry

    lax.fori_loop(0, nkt, score_tile, 0)

    def count16(cand):
        def body(j, acc):
            ind = jnp.where(half_scr[j] >= cand, jnp.int16(1), jnp.int16(0))
            for r in range(tk // PACK16):
                acc = acc + ind[r * PACK16:(r + 1) * PACK16, :]
            return acc

        acc = lax.fori_loop(0, nkt, body, jnp.zeros((PACK16, tq), jnp.int16))
        return jnp.sum(acc.astype(jnp.int32), axis=0, keepdims=True)

    def bisect16(target):
        def bit_step(s, ans):
            cand = ans + jnp.left_shift(jnp.int32(1), 15 - s)
            return jnp.where(count16(cand.astype(jnp.int16)) >= target, cand, ans)

        return lax.fori_loop(0, 16, bit_step, jnp.full((1, tq), -HALF, jnp.int32))

    thr_hi = bisect16(n_sel)
    above = jnp.where(thr_hi < HALF - 1, count16(jnp.minimum(thr_hi + 1, HALF - 1).astype(jnp.int16)), 0)

    def low_tile(j, carry):
        key = key_scr[j]
        low = jnp.bitwise_and(key, 2 * HALF - 1) - HALF
        in_bucket = lax.shift_right_arithmetic(key, 16) == thr_hi
        half_scr[j] = jnp.where(in_bucket, low, -HALF).astype(jnp.int16)
        return carry

    lax.fori_loop(0, nkt, low_tile, 0)
    thr = thr_hi * (2 * HALF) + (bisect16(n_sel - above) + HALF)

    def count(pred):
        def body(j, acc):
            ind = pred(key_scr[j], j)
            return acc + jnp.sum(ind.reshape(tk // SUBLANES, SUBLANES, tq), axis=0)

        acc = lax.fori_loop(0, nkt, body, jnp.zeros((SUBLANES, tq), jnp.int32))
        return jnp.sum(acc, axis=0, keepdims=True)

    c_gt = count(lambda kt, j: jnp.where(kt > thr, 1, 0))
    c_eq = count(lambda kt, j: jnp.where(kt == thr, 1, 0))
    need = n_sel - c_gt
    j_scr[...] = jnp.full((1, tq), INT_MAX, jnp.int32)

    @pl.when(jnp.max(c_eq - need) > 0)
    def _():
        def idx_step(s, jv):
            cand = jv + jnp.left_shift(jnp.int32(1), idx_bits - 1 - s)

            def pred(kt, j):
                idx = j * tk + lax.broadcasted_iota(jnp.int32, (tk, tq), 0)
                return jnp.where(kt == thr, jnp.where(idx < cand, 1, 0), 0)

            return jnp.where(count(pred) < need, cand, jv)

        j_scr[...] = lax.fori_loop(0, idx_bits, idx_step, jnp.zeros((1, tq), jnp.int32))

    short = thr <= KEY_NEG_INF
    lo = jnp.where(short, KEY_NEG_INF, thr)
    j_lim = jnp.where(short, -1, j_scr[...])

    _softmax_init(m_scr, acc_scr)

    def attend_tile(j, carry):
        ks = pl.multiple_of(j * tk, tk)
        kt = key_scr[j]
        idx = ks + lax.broadcasted_iota(jnp.int32, (tk, tq), 0)
        sel = jnp.where(kt > lo, 1, jnp.where(kt == lo, jnp.where(idx <= j_lim, 1, 0), 0))
        sel = jnp.where(kt < KEY_POS_INF, sel, 0)
        mask_bias = jnp.where(sel == 1, 0.0, NEG)
        kk = kk_ref[0, pl.ds(ks, tk), :]

        def score(h):
            return lax.dot_general(kk, dqa_ref[0, :, h * HS:(h + 1) * HS], NT_DIMS,
                                   preferred_element_type=F32) + mask_bias

        _attend_tile(score, lambda h: vt_ref[0, j, (h // kv_rep) * VR:(h // kv_rep + 1) * VR, :],
                     m_scr, acc_scr, s_scr, tq)
        return carry

    lax.fori_loop(0, nkt, attend_tile, 0)
    _softmax_finish(o_ref, acc_scr)


def _dsa(dqa, iqa, iw_t, kk, vt, ik, *, t_len, kv_len):
    b, tp, _ = dqa.shape
    lp = kk.shape[1]
    tk = vt.shape[3]
    tq = min(256, tp)
    n_sel = min(TOPK_MAX, kv_len // 4)
    assert tk >= n_sel and lp % tk == 0 and tp % tq == 0
    nkt_max = lp // tk
    kern = functools.partial(_dsa_kernel, tq=tq, tk=tk, off=kv_len - t_len, kv_len=kv_len, n_sel=n_sel,
                             idx_bits=max(1, int(lp - 1).bit_length()))
    whole = lambda a: pl.BlockSpec((1,) + a.shape[1:], lambda bb, i: (bb,) + (0,) * (a.ndim - 1),
                                   pipeline_mode=pl.Buffered(1))
    return pl.pallas_call(
        kern,
        grid=(b, tp // tq),
        in_specs=[pl.BlockSpec((1, tq, QW), lambda bb, i: (bb, i, 0)),
                  pl.BlockSpec((1, tq, QW), lambda bb, i: (bb, i, 0)),
                  pl.BlockSpec((1, HEADS, tq), lambda bb, i: (bb, 0, i)),
                  whole(kk), whole(vt), whole(ik)],
        out_specs=pl.BlockSpec((1, tq, BRANCH_W), lambda bb, i: (bb, i, 0)),
        out_shape=jax.ShapeDtypeStruct((b, tp, BRANCH_W), F32),
        scratch_shapes=[pltpu.VMEM((nkt_max, tk, tq), jnp.int32), pltpu.VMEM((nkt_max, tk, tq), jnp.int16),
                        pltpu.VMEM((1, tq), jnp.int32),
                        pltpu.VMEM((1, HEADS * tq), F32), pltpu.VMEM((HEADS * VR, tq), F32),
                        pltpu.VMEM((tk, HEADS * tq), F32)],
        compiler_params=_cparams("parallel", "arbitrary"),
        name="dsa",
    )(dqa, iqa, iw_t, kk, vt, ik)


def _merge_kernel(x_ref, oa_ref, ob_ref, oc_ref, gz_ref, wa_ref, wb_ref, wc_ref, wo_ref, o_ref):
    d = x_ref.shape[1]
    merged = jnp.zeros(x_ref.shape, F32)
    for n, (o_br, w_br) in enumerate(((oa_ref, wa_ref), (ob_ref, wb_ref), (oc_ref, wc_ref))):
        gate = jax.nn.sigmoid(gz_ref[:, n * d:(n + 1) * d])
        merged = merged + gate * jnp.dot(o_br[...].astype(BF16), w_br[...], preferred_element_type=F32)
    o_ref[...] = x_ref[...] + jnp.dot(merged.astype(BF16), wo_ref[...], preferred_element_type=F32)


def _merge(x, oa, ob, oc, z, wa, wb, wc, wo):
    m, d = x.shape
    tm = _row_tile(m)
    row = lambda w: pl.BlockSpec((tm, w), lambda i: (i, 0))
    full = lambda a: pl.BlockSpec(a.shape, lambda i: (0, 0))
    return pl.pallas_call(
        _merge_kernel,
        grid=(m // tm,),
        in_specs=[row(d), row(BRANCH_W), row(BRANCH_W), row(BRANCH_W), row(3 * d), full(wa), full(wb), full(wc), full(wo)],
        out_specs=row(d),
        out_shape=jax.ShapeDtypeStruct((m, d), F32),
        compiler_params=_cparams("parallel"),
        name="merge",
    )(x, oa, ob, oc, z, wa, wb, wc, wo)


def _ffn_kernel(x_ref, g_ref, wg_ref, wu_ref, wd_ref, o_ref, hn_scr, acc_scr):
    j = pl.program_id(1)

    @pl.when(j == 0)
    def _():
        x = x_ref[...]
        hn_scr[...] = (x * lax.rsqrt(jnp.mean(x * x, axis=-1, keepdims=True) + EPS) * g_ref[...]).astype(BF16)
        acc_scr[...] = jnp.zeros(acc_scr.shape, F32)

    hn = hn_scr[...]
    a = jnp.dot(hn, wg_ref[...], preferred_element_type=F32)
    u = jnp.dot(hn, wu_ref[...], preferred_element_type=F32)
    acc_scr[...] += jnp.dot((jax.nn.silu(a) * u).astype(BF16), wd_ref[...], preferred_element_type=F32)

    @pl.when(j == pl.num_programs(1) - 1)
    def _():
        o_ref[...] = x_ref[...] + acc_scr[...]


def _ffn(x, g, wg, wu, wd):
    m, d = x.shape
    dff = wg.shape[1]
    tm = _row_tile(m)
    tf = dff // 2 if (dff // 2) % LANES == 0 else dff
    return pl.pallas_call(
        _ffn_kernel,
        grid=(m // tm, dff // tf),
        in_specs=[pl.BlockSpec((tm, d), lambda i, j: (i, 0)),
                  pl.BlockSpec((1, d), lambda i, j: (0, 0)),
                  pl.BlockSpec((d, tf), lambda i, j: (0, j)),
                  pl.BlockSpec((d, tf), lambda i, j: (0, j)),
                  pl.BlockSpec((tf, d), lambda i, j: (j, 0))],
        out_specs=pl.BlockSpec((tm, d), lambda i, j: (i, 0)),
        out_shape=jax.ShapeDtypeStruct((m, d), F32),
        scratch_shapes=[pltpu.VMEM((tm, d), BF16), pltpu.VMEM((tm, d), F32)],
        compiler_params=_cparams("parallel", "arbitrary"),
        name="ffn",
    )(x, g.reshape(1, d).astype(F32), wg, wu, wd)


def _rmsnorm_kernel(x_ref, g_ref, o_ref):
    x = x_ref[...]
    o_ref[...] = x * lax.rsqrt(jnp.mean(x * x, axis=-1, keepdims=True) + EPS) * g_ref[...]


def _rmsnorm(x, g):
    m, d = x.shape
    tm = _row_tile(m)
    return pl.pallas_call(
        _rmsnorm_kernel,
        grid=(m // tm,),
        in_specs=[pl.BlockSpec((tm, d), lambda i: (i, 0)), pl.BlockSpec((1, d), lambda i: (0, 0))],
        out_specs=pl.BlockSpec((tm, d), lambda i: (i, 0)),
        out_shape=jax.ShapeDtypeStruct((m, d), F32),
        compiler_params=_cparams("parallel"),
        name="final_rmsnorm",
    )(x, g.reshape(1, d).astype(F32))


def _head_strided(w, lane_of_head=lambda h: 0):
    lead = w.shape[:-1]
    w = w.reshape(lead + (HEADS, HEAD_DIM))
    zeros = jnp.zeros(lead + (HS - HEAD_DIM,), w.dtype)
    parts = []
    for h in range(HEADS):
        o = lane_of_head(h)
        assert o in (0, HS - HEAD_DIM)
        parts += [w[..., h, :], zeros] if o == 0 else [zeros, w[..., h, :]]
    return jnp.concatenate(parts, axis=-1)


def _pack_weights(w_in, w_mla_uq, w_mla_ukv):
    sizes = (BRANCH_W, BRANCH_W, BRANCH_W, HEADS, 256, 128, MLA_ROPE, BRANCH_W, 128, 128, BRANCH_W, HEAD_DIM, HEADS)
    offs = np.concatenate([[0], np.cumsum(sizes)])
    (fq, fk, fv, ff, cq, ckv, kr, dq, dk, dv, iq, ik, iw) = [w_in[..., int(offs[n]):int(offs[n + 1])]
                                                            for n in range(len(sizes))]
    gz = w_in[..., int(offs[-1]):]
    pad = jnp.zeros(w_in.shape[:-1] + (LANES - (HEAD_DIM + MLA_ROPE + 2 * HEADS),), w_in.dtype)
    kv_rep = HEADS // DSA_KV_HEADS
    w_in_p = jnp.concatenate(
        [gz, _head_strided(fq), _head_strided(fk), _head_strided(dq, lambda h: (h // kv_rep) * HEAD_DIM),
         _head_strided(iq), fv, fk, cq, ckv, dk, dv, ik, kr, ff, iw, pad], axis=-1).astype(BF16)
    assert w_in_p.shape[-1] == N_PACK

    depth = w_in.shape[0]
    uq = w_mla_uq.reshape(depth, w_mla_uq.shape[1], HEADS, HEAD_DIM + MLA_ROPE)
    uq = jnp.concatenate([uq, jnp.zeros(uq.shape[:-1] + (HS - HEAD_DIM - MLA_ROPE,), uq.dtype)], axis=-1)
    w_uq_p = uq.reshape(depth, -1, QW).astype(BF16)
    ukv = w_mla_ukv.reshape(depth, w_mla_ukv.shape[1], HEADS, 2 * HEAD_DIM)
    w_uk_p = _head_strided(ukv[..., :HEAD_DIM].reshape(depth, -1, BRANCH_W)).astype(BF16)
    w_uv_p = ukv[..., HEAD_DIM:].reshape(depth, -1, BRANCH_W).astype(BF16)
    return w_in_p, w_uq_p, w_uk_p, w_uv_p


def _with_past(past, new, lp):
    parts = [new] if past is None else [past.astype(new.dtype), new]
    rows = sum(p.shape[1] for p in parts)
    if lp > rows:
        parts.append(jnp.zeros((new.shape[0], lp - rows, new.shape[2]), new.dtype))
    return parts[0] if len(parts) == 1 else jnp.concatenate(parts, axis=1)


def _pad_axis(a, axis, size):
    if a.shape[axis] == size:
        return a
    widths = [(0, 0)] * a.ndim
    widths[axis] = (0, size - a.shape[axis])
    return jnp.pad(a, widths)


def _layer(x, past, tabs, p, layer, depth, bufs):
    b, t_len, d = x.shape
    m = b * t_len
    p_len = 0 if past is None else past[0].shape[1]
    kv_len = p_len + t_len
    tk = min(512, _round_up(kv_len, LANES))
    lp = _round_up(kv_len, tk)
    tp = _round_up(t_len, LANES)
    x2 = x.reshape(m, d)

    z = _rms_matmul(x2, 0, d, p["norm_attn"], p["w_in"], tn=N_PACK // 7)
    qm = _rms_matmul(z, C_CQ // 256, 256, p["norm_mla_q"], p["w_uq"], tn=256)
    fqa, dqa, iqa, mqa, ckv_n, dk_r, misc = _post(z, qm, tabs, p["b_forget"], p["norm_mla_kv"], t_len)

    z3 = z.reshape(b, t_len, N_PACK)
    r3 = lambda a: a.reshape(b, t_len, a.shape[-1])
    bufs = _write_state(layer, depth, bufs, z, ckv_n, dk_r, misc)
    fv = z3[..., C_FV:C_FV + BRANCH_W]
    dv = z3[..., C_DV:C_DV + LANES]
    misc3, ckv3, dk3 = r3(misc), r3(ckv_n), r3(dk_r)
    logf = misc3[..., M_FF:M_FF + HEADS]

    if past is None:
        fox_k, fox_v = (z3, C_FK // QW, QW), (z3, C_FV // BRANCH_W, BRANCH_W)
        dsa_k, dsa_v = (dk3, 0, LANES), (z3, C_DV // LANES, LANES)
        logf_all, ckv_all, kpe_all = logf, ckv3, misc3
        ik_all = misc3.astype(BF16)
    else:
        (c_fk, c_fv, c_logf, c_ckv, c_kpe, c_dk, c_dv, c_ik) = past
        flat = lambda a: a.reshape(b, p_len, -1)
        c_fk_s = _pad_axis(c_fk, 3, HS).reshape(b, p_len, QW)
        fox_k = (_with_past(c_fk_s, z3[..., C_FK:C_FK + QW], lp), 0, QW)
        fox_v = (_with_past(flat(c_fv), fv, lp), 0, BRANCH_W)
        dsa_k = (_with_past(flat(c_dk), dk3, lp), 0, LANES)
        dsa_v = (_with_past(flat(c_dv), dv, lp), 0, LANES)
        logf_all = _with_past(c_logf, logf, lp)
        ckv_all = _with_past(c_ckv, ckv3, lp)
        c_kpe_s = jnp.pad(c_kpe, ((0, 0), (0, 0), (M_KR, LANES - M_KR - MLA_ROPE)))
        kpe_all = _with_past(c_kpe_s, misc3, lp)
        ik_all = _with_past(_pad_axis(c_ik, 2, LANES), misc3, lp).astype(BF16)

    pad_q = lambda a: _pad_axis(a.reshape(b, t_len, a.shape[-1]), 1, tp)

    cum = _cumsum(_pad_axis(logf_all, 1, lp)) * LOG2E
    q_bias = _pad_axis(cum[:, :, kv_len - t_len:kv_len], 2, tp)
    fka, fvt = _kside(fox_k, fox_v, tk)
    o_a = _flash(pad_q(fqa), fka, fvt, t_len=t_len, kv_len=kv_len, per_frame=True,
                 bias=(q_bias, cum.transpose(0, 2, 1)))

    mka, mvt = _mla_kv(_pad_axis(ckv_all, 1, lp), _pad_axis(kpe_all, 1, lp), p["w_uk"], p["w_uv"], tk)
    o_b = _flash(pad_q(mqa), mka, mvt, t_len=t_len, kv_len=kv_len, per_frame=False)

    dka, dvt = _kside(dsa_k, dsa_v, tk)
    iw_t = _pad_axis(misc3[..., M_IW:M_IW + HEADS].transpose(0, 2, 1), 2, tp)
    o_c = _dsa(pad_q(dqa), pad_q(iqa), iw_t, dka, dvt, _pad_axis(ik_all, 1, lp), t_len=t_len, kv_len=kv_len)

    unpad = lambda o: o[:, :t_len].reshape(m, BRANCH_W)
    x2 = _merge(x2, unpad(o_a), unpad(o_b), unpad(o_c), z,
                p["w_br_fox"], p["w_br_mla"], p["w_br_dsa"], p["w_out"])
    x2 = _ffn(x2, p["norm_ffn"], p["w_ffn_gate"], p["w_ffn_up"], p["w_ffn_down"])
    return x2.reshape(b, t_len, d), bufs


def kernel(x_prompt, x_sample, cache_fox_k, cache_fox_v, cache_fox_logf, cache_mla_ckv, cache_mla_kpe, cache_dsa_k, cache_dsa_v, cache_dsa_idxk, norm_attn, w_in, b_forget, norm_mla_q, norm_mla_kv, w_mla_uq, w_mla_ukv, w_br_fox, w_br_mla, w_br_dsa, w_out, norm_ffn, w_ffn_gate, w_ffn_up, w_ffn_down, norm_final):
    depth = w_in.shape[0]
    past_len = cache_fox_k.shape[2]
    w_in_p, w_uq_p, w_uk_p, w_uv_p = _pack_weights(w_in, w_mla_uq, w_mla_ukv)
    bf = lambda a: a.astype(BF16)
    w_br_fox, w_br_mla, w_br_dsa, w_out = bf(w_br_fox), bf(w_br_mla), bf(w_br_dsa), bf(w_out)
    w_ffn_gate, w_ffn_up, w_ffn_down = bf(w_ffn_gate), bf(w_ffn_up), bf(w_ffn_down)
    caches = (cache_fox_k, cache_fox_v, cache_fox_logf, cache_mla_ckv, cache_mla_kpe,
              cache_dsa_k, cache_dsa_v, cache_dsa_idxk)

    tabs_p = _rope_tables(jnp.arange(x_prompt.shape[1]))
    tabs_s = _rope_tables(past_len + jnp.arange(x_sample.shape[1]))

    xp, xs = x_prompt, x_sample
    bufs_p = bufs_s = None
    for l in range(depth):
        p = dict(norm_attn=norm_attn[l], w_in=w_in_p[l], b_forget=b_forget[l], norm_mla_q=norm_mla_q[l],
                 norm_mla_kv=norm_mla_kv[l], w_uq=w_uq_p[l], w_uk=w_uk_p[l], w_uv=w_uv_p[l], w_br_fox=w_br_fox[l],
                 w_br_mla=w_br_mla[l], w_br_dsa=w_br_dsa[l], w_out=w_out[l], norm_ffn=norm_ffn[l],
                 w_ffn_gate=w_ffn_gate[l], w_ffn_up=w_ffn_up[l], w_ffn_down=w_ffn_down[l])
        xp, bufs_p = _layer(xp, None, tabs_p, p, l, depth, bufs_p)
        xs, bufs_s = _layer(xs, tuple(c[l] for c in caches), tabs_s, p, l, depth, bufs_s)

    def unflatten(bufs, like):
        bt = like.shape[:2]
        tails = ((HEADS, HEAD_DIM), (HEADS, HEAD_DIM), (HEADS,), (LANES,), (MLA_ROPE,),
                 (DSA_KV_HEADS, HEAD_DIM), (DSA_KV_HEADS, HEAD_DIM), (HEAD_DIM,))
        return [a.reshape((depth,) + bt + tail) for a, tail in zip(bufs, tails)]

    stacked_p = unflatten(bufs_p, x_prompt)
    stacked_s = unflatten(bufs_s, x_sample)
    yp = _rmsnorm(xp.reshape(-1, xp.shape[-1]), norm_final).reshape(xp.shape)
    ys = _rmsnorm(xs.reshape(-1, xs.shape[-1]), norm_final).reshape(xs.shape)
    out = [yp, ys]
    for sp, ss in zip(stacked_p, stacked_s):
        out += [sp, ss]
    return tuple(out)
```

```python
import functools
import math

import numpy as np
import jax
import jax.numpy as jnp
from jax import lax
from jax.experimental import pallas as pl
from jax.experimental.pallas import tpu as pltpu

F32 = jnp.float32
BF16 = jnp.bfloat16

CHUNK = 64
EPS = 1e-6
HEADS = 8
HEAD_DIM = 64
MLA_ROPE = 32
MLA_THETA = 10000.0
DSA_KV_HEADS = 2
ROPE_THETA = 500000.0
PART_ROT = HEAD_DIM // 4
TOPK_MAX = 256
BRANCH_W = HEADS * HEAD_DIM

LANES = 128
SUBLANES = 8
VMEM_LIMIT = 56 * 1024 * 1024
NEG = -1e30
LOG2E = math.log2(math.e)
HS = LANES
QW = HEADS * HS
VR = HEAD_DIM + 16

C_GZ = 0
C_FQ = 3072
C_FK = 4096
C_DQ = 5120
C_IQ = 6144
C_FV = 7168
C_FKC = 7680
C_CQ = 8192
C_CKV = 8448
C_DK = 8576
C_DV = 8704
C_MISC = 8832
N_PACK = 8960
M_IK, M_KR, M_FF, M_IW = 0, 64, 96, 104

KEY_NEG_INF = int(np.array(0xFF800000 ^ 0x7FFFFFFF, dtype=np.uint32).view(np.int32))
KEY_POS_INF = 0x7F800000
INT_MAX = 2 ** 31 - 1
HALF = 2 ** 15
PACK16 = 16
N_COUNTERS = 4

NT_DIMS = (((1,), (1,)), ((), ()))


def _cparams(*sem):
    return pltpu.CompilerParams(dimension_semantics=sem, vmem_limit_bytes=VMEM_LIMIT)


def _round_up(a, b):
    return -(-a // b) * b


def _row_tile(m, cap=512):
    t = cap
    while t > 8 and m % t:
        t //= 2
    assert m % t == 0
    return t


def _rms_mm_kernel(x_ref, g_ref, w_ref, o_ref, xn_ref, *, normalize):
    @pl.when(pl.program_id(1) == 0)
    def _():
        x = x_ref[...].astype(F32)
        if normalize:
            x = x * lax.rsqrt(jnp.mean(x * x, axis=-1, keepdims=True) + EPS) * g_ref[...]
        xn_ref[...] = x.astype(BF16)

    o_ref[...] = jnp.dot(xn_ref[...], w_ref[...], preferred_element_type=F32).astype(o_ref.dtype)


def _rms_matmul(x, col_block, k, g, w, *, tn, out_dtype=F32):
    m = x.shape[0]
    n = w.shape[1]
    tm = _row_tile(m, cap=1024)
    normalize = g is not None
    if g is None:
        g = jnp.ones((k,), F32)
    return pl.pallas_call(
        functools.partial(_rms_mm_kernel, normalize=normalize),
        grid=(m // tm, n // tn),
        in_specs=[pl.BlockSpec((tm, k), lambda i, j: (i, col_block)),
                  pl.BlockSpec((1, k), lambda i, j: (0, 0)),
                  pl.BlockSpec((k, tn), lambda i, j: (0, j))],
        out_specs=pl.BlockSpec((tm, tn), lambda i, j: (i, j)),
        out_shape=jax.ShapeDtypeStruct((m, n), out_dtype),
        scratch_shapes=[pltpu.VMEM((tm, k), BF16)],
        compiler_params=_cparams("parallel", "arbitrary"),
        name="rms_matmul",
    )(x, g.reshape(1, k).astype(F32), w)


N_TABS = 12
TAB_IQ, TAB_DK, TAB_DQ_HI, TAB_MLA = 0, 3, 6, 9


def _rope_tables(pos):
    lane = np.arange(LANES)

    def pattern(theta, rot, period, lane0, width):
        half = rot // 2
        inv_freq = theta ** (-jnp.arange(half, dtype=F32) * (2.0 / rot))
        ang = pos.astype(F32)[:, None] * inv_freq[None, :]
        cos, sin = jnp.cos(ang), jnp.sin(ang)
        inside = (lane >= lane0) & (lane < lane0 + width)
        r = (lane - lane0) % period
        first = inside & (r < half)
        second = inside & (r >= half) & (r < rot)
        fidx = np.where(first, r, np.where(second, r - half, 0))
        c = jnp.where((first | second)[None, :], cos[:, fidx], 1.0)
        s1 = jnp.where(first[None, :], -sin[:, fidx], 0.0)
        s2 = jnp.where(second[None, :], sin[:, fidx], 0.0)
        return [c, s1, s2]

    tabs = (pattern(ROPE_THETA, PART_ROT, HS, 0, HS)
            + pattern(ROPE_THETA, PART_ROT, HEAD_DIM, 0, LANES)
            + pattern(ROPE_THETA, PART_ROT, HEAD_DIM, HEAD_DIM, HEAD_DIM)
            + pattern(MLA_THETA, MLA_ROPE, MLA_ROPE, M_KR, MLA_ROPE))
    return jnp.stack(tabs, axis=0)


def _rot(x, tab_ref, t0, half):
    return (x * tab_ref[t0] + pltpu.roll(x, LANES - half, 1) * tab_ref[t0 + 1]
            + pltpu.roll(x, half, 1) * tab_ref[t0 + 2])


def _post_kernel(fq_ref, dq_ref, iq_ref, ckv_ref, dk_ref, misc_ref, qm_ref, tab_ref, bias_ref, gkv_ref,
                 fqa_o, dqa_o, iqa_o, mqa_o, ckv_o, dk_o, misc_o):
    half = PART_ROT // 2
    kv_rep = HEADS // DSA_KV_HEADS
    sc_dot = (HEAD_DIM ** -0.5) * LOG2E
    sc_mla = ((HEAD_DIM + MLA_ROPE) ** -0.5) * LOG2E
    for h in range(HEADS):
        sl = slice(h * HS, (h + 1) * HS)
        fqa_o[:, sl] = (fq_ref[:, sl] * sc_dot).astype(BF16)
        t_dq = TAB_IQ if h < kv_rep else TAB_DQ_HI
        dqa_o[:, sl] = (_rot(dq_ref[:, sl], tab_ref, t_dq, half) * sc_dot).astype(BF16)
        iqa_o[:, sl] = _rot(iq_ref[:, sl], tab_ref, TAB_IQ, half).astype(BF16)
        mqa_o[:, sl] = (_rot(qm_ref[:, sl], tab_ref, TAB_MLA, MLA_ROPE // 2) * sc_mla).astype(BF16)
    dk_o[...] = _rot(dk_ref[...], tab_ref, TAB_DK, half)

    ckv = ckv_ref[...]
    ckv_o[...] = ckv * lax.rsqrt(jnp.mean(ckv * ckv, axis=-1, keepdims=True) + EPS) * gkv_ref[...]

    x = misc_ref[...]
    roped = (x * (tab_ref[TAB_IQ] * tab_ref[TAB_MLA])
             + pltpu.roll(x, LANES - half, 1) * tab_ref[TAB_IQ + 1] + pltpu.roll(x, half, 1) * tab_ref[TAB_IQ + 2]
             + pltpu.roll(x, LANES - MLA_ROPE // 2, 1) * tab_ref[TAB_MLA + 1]
             + pltpu.roll(x, MLA_ROPE // 2, 1) * tab_ref[TAB_MLA + 2])
    lane = lax.broadcasted_iota(jnp.int32, x.shape, 1)
    logf = jax.nn.log_sigmoid(x + bias_ref[...])
    is_ff = jnp.where(lane >= M_FF, jnp.where(lane < M_IW, 1, 0), 0)
    misc_o[...] = jnp.where(is_ff == 1, logf, roped)


def _post(z, qm, tabs, b_forget, g_kv, t_len):
    m = z.shape[0]
    tm = _row_tile(m)
    if t_len % tm == 0:
        nt = t_len // tm
        tab_map = lambda i: (0, i % nt, 0)
    else:
        assert tm % t_len == 0
        tabs = jnp.tile(tabs, (1, tm // t_len, 1))
        tab_map = lambda i: (0, 0, 0)
    bias = jnp.zeros((1, LANES), F32).at[0, M_FF:M_FF + HEADS].set(b_forget.astype(F32))
    blk = lambda w, c0: pl.BlockSpec((tm, w), lambda i: (i, c0 // w))
    outs = [(QW, BF16), (QW, BF16), (QW, BF16), (QW, BF16), (LANES, F32), (LANES, F32), (LANES, F32)]
    return pl.pallas_call(
        _post_kernel,
        grid=(m // tm,),
        in_specs=[blk(QW, C_FQ), blk(QW, C_DQ), blk(QW, C_IQ), blk(LANES, C_CKV), blk(LANES, C_DK),
                  blk(LANES, C_MISC), pl.BlockSpec((tm, QW), lambda i: (i, 0)),
                  pl.BlockSpec((N_TABS, tm, LANES), tab_map),
                  pl.BlockSpec((1, LANES), lambda i: (0, 0)),
                  pl.BlockSpec((1, LANES), lambda i: (0, 0))],
        out_specs=[pl.BlockSpec((tm, w), lambda i: (i, 0)) for w, _ in outs],
        out_shape=[jax.ShapeDtypeStruct((m, w), dt) for w, dt in outs],
        compiler_params=_cparams("parallel"),
        name="post",
    )(z, z, z, z, z, z, qm, tabs, bias, g_kv.reshape(1, LANES).astype(F32))


STATE_WIDTHS = (BRANCH_W, BRANCH_W, HEADS, LANES, MLA_ROPE, LANES, LANES, HEAD_DIM)


def _state_kernel(*refs):
    n = len(STATE_WIDTHS)
    fk_ref, fv_ref, dv_ref, ckv_ref, dk_ref, misc_ref = refs[:6]
    fk_o, fv_o, logf_o, ckv_o, kpe_o, dk_o, dv_o, ik_o = refs[len(refs) - n:]
    fk_o[0] = fk_ref[...]
    fv_o[0] = fv_ref[...]
    logf_o[0] = misc_ref[:, M_FF:M_FF + HEADS]
    ckv_o[0] = ckv_ref[...]
    kpe_o[0] = misc_ref[:, M_KR:M_KR + MLA_ROPE]
    dk_o[0] = dk_ref[...]
    dv_o[0] = dv_ref[...]
    ik_o[0] = misc_ref[:, M_IK:M_IK + HEAD_DIM]


def _write_state(layer, depth, bufs, z, ckv_n, dk_r, misc):
    m = z.shape[0]
    tm = _row_tile(m)
    n = len(STATE_WIDTHS)
    blk = lambda w, c0: pl.BlockSpec((tm, w), lambda i: (i, c0 // w))
    in_specs = [blk(BRANCH_W, C_FKC), blk(BRANCH_W, C_FV), blk(LANES, C_DV),
                pl.BlockSpec((tm, LANES), lambda i: (i, 0)), pl.BlockSpec((tm, LANES), lambda i: (i, 0)),
                pl.BlockSpec((tm, LANES), lambda i: (i, 0))]
    operands = [z, z, z, ckv_n, dk_r, misc]
    aliases = {}
    if bufs is not None:
        in_specs += [pl.BlockSpec(memory_space=pl.ANY)] * n
        aliases = {len(operands) + k: k for k in range(n)}
        operands += list(bufs)
    return pl.pallas_call(
        _state_kernel,
        grid=(m // tm,),
        in_specs=in_specs,
        out_specs=[pl.BlockSpec((1, tm, w), lambda i: (layer, i, 0)) for w in STATE_WIDTHS],
        out_shape=[jax.ShapeDtypeStruct((depth, m, w), F32) for w in STATE_WIDTHS],
        input_output_aliases=aliases,
        compiler_params=_cparams("arbitrary"),
        name="state",
    )(*operands)


def _cumsum_kernel(x_ref, o_ref):
    nc = x_ref.shape[1]
    row = lax.broadcasted_iota(jnp.int32, (LANES, LANES), 0)
    col = lax.broadcasted_iota(jnp.int32, (LANES, LANES), 1)
    tri = jnp.where(row <= col, 1.0, 0.0).astype(F32)

    def body(c, carry):
        cs = jnp.dot(x_ref[0, c], tri, precision=lax.Precision.HIGHEST, preferred_element_type=F32) + carry
        o_ref[0, c] = cs
        return cs[:, LANES - 1:LANES]

    lax.fori_loop(0, nc, body, jnp.zeros((HEADS, 1), F32))


def _cumsum(logf):
    b, lp, h = logf.shape
    nc = lp // LANES
    x = logf.transpose(0, 2, 1).reshape(b, h, nc, LANES).transpose(0, 2, 1, 3)
    out = pl.pallas_call(
        _cumsum_kernel,
        grid=(b,),
        in_specs=[pl.BlockSpec((1, nc, h, LANES), lambda i: (i, 0, 0, 0))],
        out_specs=pl.BlockSpec((1, nc, h, LANES), lambda i: (i, 0, 0, 0)),
        out_shape=jax.ShapeDtypeStruct((b, nc, h, LANES), F32),
        compiler_params=_cparams("parallel"),
        name="cumsum",
    )(x)
    return out.transpose(0, 2, 1, 3).reshape(b, h, lp)


def _store_value_tiles(vt_o, v):
    vt = v.T
    tk = v.shape[0]
    for h in range(v.shape[1] // HEAD_DIM):
        vt_o[0, 0, h * VR:h * VR + HEAD_DIM, :] = vt[h * HEAD_DIM:(h + 1) * HEAD_DIM, :].astype(BF16)
        vt_o[0, 0, h * VR + HEAD_DIM:(h + 1) * VR, :] = jnp.ones((VR - HEAD_DIM, tk), BF16)


def _kside_kernel(k_ref, v_ref, ka_o, vt_o):
    ka_o[0] = k_ref[0].astype(BF16)
    _store_value_tiles(vt_o, v_ref[0])


def _kside(k, v, tk):
    (ka, kcb, kw), (va, vcb, vw) = k, v
    b, lp = ka.shape[0], ka.shape[1]
    nkt = lp // tk
    vr = vw // HEAD_DIM * VR
    return pl.pallas_call(
        _kside_kernel,
        grid=(b, nkt),
        in_specs=[pl.BlockSpec((1, tk, kw), lambda bb, j: (bb, j, kcb)),
                  pl.BlockSpec((1, tk, vw), lambda bb, j: (bb, j, vcb))],
        out_specs=[pl.BlockSpec((1, tk, kw), lambda bb, j: (bb, j, 0)),
                   pl.BlockSpec((1, 1, vr, tk), lambda bb, j: (bb, j, 0, 0))],
        out_shape=[jax.ShapeDtypeStruct((b, lp, kw), BF16), jax.ShapeDtypeStruct((b, nkt, vr, tk), BF16)],
        compiler_params=_cparams("parallel", "parallel"),
        name="kside",
    )(ka, va)


def _mla_kv_kernel(ckv_ref, kpe_ref, wk_ref, wv_ref, ka_o, vt_o):
    c = ckv_ref[0].astype(BF16)
    k = jnp.dot(c, wk_ref[...], preferred_element_type=F32)
    kpe = kpe_ref[0]
    lane = lax.broadcasted_iota(jnp.int32, kpe.shape, 1)
    kpe = jnp.where(lane >= M_KR, jnp.where(lane < M_KR + MLA_ROPE, kpe, 0.0), 0.0)
    for h in range(HEADS):
        sl = slice(h * HS, (h + 1) * HS)
        ka_o[0, :, sl] = (k[:, sl] + kpe).astype(BF16)
    _store_value_tiles(vt_o, jnp.dot(c, wv_ref[...], preferred_element_type=F32))


def _mla_kv(ckv_all, kpe_all, wk, wv, tk):
    b, lp, _ = ckv_all.shape
    nkt = lp // tk
    return pl.pallas_call(
        _mla_kv_kernel,
        grid=(b, nkt),
        in_specs=[pl.BlockSpec((1, tk, LANES), lambda bb, j: (bb, j, 0)),
                  pl.BlockSpec((1, tk, LANES), lambda bb, j: (bb, j, 0)),
                  pl.BlockSpec(wk.shape, lambda bb, j: (0, 0)),
                  pl.BlockSpec(wv.shape, lambda bb, j: (0, 0))],
        out_specs=[pl.BlockSpec((1, tk, QW), lambda bb, j: (bb, j, 0)),
                   pl.BlockSpec((1, 1, HEADS * VR, tk), lambda bb, j: (bb, j, 0, 0))],
        out_shape=[jax.ShapeDtypeStruct((b, lp, QW), BF16), jax.ShapeDtypeStruct((b, nkt, HEADS * VR, tk), BF16)],
        compiler_params=_cparams("parallel", "parallel"),
        name="mla_kv",
    )(ckv_all, kpe_all, wk, wv)


def _attend_tile(score_fn, vt_fn, m_scr, acc_scr, s_scr, tq):
    for h in range(HEADS):
        s_scr[:, h * tq:(h + 1) * tq] = score_fn(h)
    s_all = s_scr[...]
    m_prev = m_scr[...]
    m_new = jnp.maximum(m_prev, jnp.max(s_all, axis=0, keepdims=True))
    alpha = jnp.exp2(m_prev - m_new)
    m_scr[...] = m_new
    p_all = jnp.exp2((s_all - m_new).astype(BF16))
    for h in range(HEADS):
        rows = slice(h * VR, (h + 1) * VR)
        cols = slice(h * tq, (h + 1) * tq)
        acc_scr[rows, :] = alpha[:, cols] * acc_scr[rows, :] + jnp.dot(vt_fn(h), p_all[:, cols],
                                                                     preferred_element_type=F32)


def _softmax_init(m_scr, acc_scr):
    m_scr[...] = jnp.full(m_scr.shape, NEG, F32)
    acc_scr[...] = jnp.zeros(acc_scr.shape, F32)


def _softmax_finish(o_ref, acc_scr):
    outs = [acc_scr[h * VR:h * VR + HEAD_DIM, :] / acc_scr[h * VR + HEAD_DIM:h * VR + HEAD_DIM + 1, :]
            for h in range(HEADS)]
    o_ref[0] = jnp.concatenate(outs, axis=0).T


def _last_allowed(q_pos, per_frame):
    return q_pos if per_frame else (q_pos | (CHUNK - 1))


def _flash_kernel(*refs, tq, tk, nk, off, kv_len, per_frame, has_bias):
    if has_bias:
        i_tab, j_tab, qa_ref, ka_ref, vt_ref, qb_ref, kb_ref, o_ref, m_scr, acc_scr, s_scr = refs
    else:
        i_tab, j_tab, qa_ref, ka_ref, vt_ref, o_ref, m_scr, acc_scr, s_scr = refs
    i, j = i_tab[pl.program_id(1)], j_tab[pl.program_id(1)]
    q0 = i * tq + off
    k0 = j * tk
    k_last = k0 + tk - 1

    @pl.when(j == 0)
    def _():
        _softmax_init(m_scr, acc_scr)

    def step(masked):
        if masked:
            kpos = k0 + lax.broadcasted_iota(jnp.int32, (tk, tq), 0)
            qlim = _last_allowed(q0 + lax.broadcasted_iota(jnp.int32, (tk, tq), 1), per_frame)
            mask_bias = jnp.where(kpos <= qlim, jnp.where(kpos < kv_len, 0.0, NEG), NEG)

        def score(h):
            sl = slice(h * HS, (h + 1) * HS)
            s_t = lax.dot_general(ka_ref[0, :, sl], qa_ref[0, :, sl], NT_DIMS, preferred_element_type=F32)
            if has_bias:
                s_t = s_t + (qb_ref[0, h:h + 1, :] - kb_ref[0, :, h:h + 1])
            return s_t + mask_bias if masked else s_t

        _attend_tile(score, lambda h: vt_ref[0, 0, h * VR:(h + 1) * VR, :], m_scr, acc_scr, s_scr, tq)

    clear = jnp.logical_and(k_last <= _last_allowed(q0, per_frame), k_last < kv_len)

    @pl.when(clear)
    def _():
        step(False)

    @pl.when(jnp.logical_not(clear))
    def _():
        step(True)

    @pl.when(j == jnp.minimum(_last_allowed(q0 + tq - 1, per_frame) // tk, nk - 1))
    def _():
        _softmax_finish(o_ref, acc_scr)


def _flash(qa, ka, vt, *, t_len, kv_len, per_frame, bias=None):
    b, tp, _ = qa.shape
    lp = ka.shape[1]
    tk = vt.shape[3]
    tq = min(512, tp)
    off = kv_len - t_len
    nq, nk = tp // tq, lp // tk

    pairs = [(i, j) for i in range(nq)
             for j in range(min(_last_allowed(i * tq + off + tq - 1, per_frame) // tk, nk - 1) + 1)]
    i_tab = jnp.asarray([p[0] for p in pairs], jnp.int32)
    j_tab = jnp.asarray([p[1] for p in pairs], jnp.int32)

    operands = [qa, ka, vt]
    in_specs = [pl.BlockSpec((1, tq, QW), lambda bb, p, it, jt: (bb, it[p], 0)),
                pl.BlockSpec((1, tk, QW), lambda bb, p, it, jt: (bb, jt[p], 0)),
                pl.BlockSpec((1, 1, HEADS * VR, tk), lambda bb, p, it, jt: (bb, jt[p], 0, 0))]
    if bias is not None:
        operands += list(bias)
        in_specs += [pl.BlockSpec((1, HEADS, tq), lambda bb, p, it, jt: (bb, 0, it[p])),
                     pl.BlockSpec((1, tk, HEADS), lambda bb, p, it, jt: (bb, jt[p], 0))]
    kern = functools.partial(_flash_kernel, tq=tq, tk=tk, nk=nk, off=off, kv_len=kv_len, per_frame=per_frame,
                             has_bias=bias is not None)
    return pl.pallas_call(
        kern,
        grid_spec=pltpu.PrefetchScalarGridSpec(
            num_scalar_prefetch=2,
            grid=(b, len(pairs)),
            in_specs=in_specs,
            out_specs=pl.BlockSpec((1, tq, BRANCH_W), lambda bb, p, it, jt: (bb, it[p], 0)),
            scratch_shapes=[pltpu.VMEM((1, HEADS * tq), F32), pltpu.VMEM((HEADS * VR, tq), F32),
                            pltpu.VMEM((tk, HEADS * tq), F32)]),
        out_shape=jax.ShapeDtypeStruct((b, tp, BRANCH_W), F32),
        compiler_params=_cparams("parallel", "arbitrary"),
        name="flash_fox" if per_frame else "flash_mla",
    )(i_tab, j_tab, *operands)


def _sortable(x):
    bits = pltpu.bitcast(x, jnp.int32)
    return jnp.where(bits < 0, bits ^ INT_MAX, bits)


def _dsa_kernel(dqa_ref, iqa_ref, iw_ref, kk_ref, vt_ref, ik_ref, o_ref, key_scr, half_scr, j_scr, m_scr, acc_scr, s_scr,
                *, tq, tk, off, kv_len, n_sel, idx_bits):
    i = pl.program_id(1)
    q0 = i * tq + off
    qlim = (q0 + lax.broadcasted_iota(jnp.int32, (1, tq), 1)) | (CHUNK - 1)
    k_end = jnp.minimum(((q0 + tq - 1) | (CHUNK - 1)) + 1, kv_len)
    nkt = (k_end + tk - 1) // tk
    kv_rep = HEADS // DSA_KV_HEADS

    def score_tile(j, carry):
        ks = pl.multiple_of(j * tk, tk)
        ik = ik_ref[0, pl.ds(ks, tk), :]
        sc = jnp.zeros((tk, tq), F32)
        for h in range(HEADS):
            s = lax.dot_general(ik, iqa_ref[0, :, h * HS:(h + 1) * HS], NT_DIMS, preferred_element_type=F32)
            sc = sc + jnp.maximum(s, 0.0) * iw_ref[0, h:h + 1, :]
        kpos = ks + lax.broadcasted_iota(jnp.int32, (tk, tq), 0)
        adm = jnp.where(kpos <= qlim, jnp.where(kpos < kv_len, 1, 0), 0)
        key = jnp.where(adm == 1, _sortable(sc), KEY_NEG_INF)
        key_scr[j] = key
        half_scr[j] = lax.shift_right_arithmetic(key, 16).astype(jnp.int16)
        return carry

    lax.fori_loop(0, nkt, score_tile, 0)

    def count16(cand):
        def body(j, accs):
            ind = jnp.where(half_scr[j] >= cand, jnp.int16(1), jnp.int16(0))
            accs = list(accs)
            for r in range(tk // PACK16):
                accs[r % N_COUNTERS] = accs[r % N_COUNTERS] + ind[r * PACK16:(r + 1) * PACK16, :]
            return tuple(accs)

        accs = lax.fori_loop(0, nkt, body, (jnp.zeros((PACK16, tq), jnp.int16),) * N_COUNTERS)
        total = sum(a.astype(jnp.int32) for a in accs)
        return jnp.sum(total, axis=0, keepdims=True)

    def bisect16(target):
        def bit_step(s, ans):
            cand = ans + jnp.left_shift(jnp.int32(1), 15 - s)
            return jnp.where(count16(cand.astype(jnp.int16)) >= target, cand, ans)

        return lax.fori_loop(0, 16, bit_step, jnp.full((1, tq), -HALF, jnp.int32))

    thr_hi = bisect16(n_sel)
    above = jnp.where(thr_hi < HALF - 1, count16(jnp.minimum(thr_hi + 1, HALF - 1).astype(jnp.int16)), 0)

    def low_tile(j, carry):
        key = key_scr[j]
        low = jnp.bitwise_and(key, 2 * HALF - 1) - HALF
        in_bucket = lax.shift_right_arithmetic(key, 16) == thr_hi
        half_scr[j] = jnp.where(in_bucket, low, -HALF).astype(jnp.int16)
        return carry

    lax.fori_loop(0, nkt, low_tile, 0)
    thr = thr_hi * (2 * HALF) + (bisect16(n_sel - above) + HALF)

    def count(pred):
        def body(j, acc):
            ind = pred(key_scr[j], j)
            return acc + jnp.sum(ind.reshape(tk // SUBLANES, SUBLANES, tq), axis=0)

        acc = lax.fori_loop(0, nkt, body, jnp.zeros((SUBLANES, tq), jnp.int32))
        return jnp.sum(acc, axis=0, keepdims=True)

    c_gt = count(lambda kt, j: jnp.where(kt > thr, 1, 0))
    c_eq = count(lambda kt, j: jnp.where(kt == thr, 1, 0))
    need = n_sel - c_gt
    j_scr[...] = jnp.full((1, tq), INT_MAX, jnp.int32)

    @pl.when(jnp.max(c_eq - need) > 0)
    def _():
        def idx_step(s, jv):
            cand = jv + jnp.left_shift(jnp.int32(1), idx_bits - 1 - s)

            def pred(kt, j):
                idx = j * tk + lax.broadcasted_iota(jnp.int32, (tk, tq), 0)
                return jnp.where(kt == thr, jnp.where(idx < cand, 1, 0), 0)

            return jnp.where(count(pred) < need, cand, jv)

        j_scr[...] = lax.fori_loop(0, idx_bits, idx_step, jnp.zeros((1, tq), jnp.int32))

    short = thr <= KEY_NEG_INF
    lo = jnp.where(short, KEY_NEG_INF, thr)
    j_lim = jnp.where(short, -1, j_scr[...])

    _softmax_init(m_scr, acc_scr)

    def attend_tile(j, carry):
        ks = pl.multiple_of(j * tk, tk)
        kt = key_scr[j]
        idx = ks + lax.broadcasted_iota(jnp.int32, (tk, tq), 0)
        sel = jnp.where(kt > lo, 1, jnp.where(kt == lo, jnp.where(idx <= j_lim, 1, 0), 0))
        sel = jnp.where(kt < KEY_POS_INF, sel, 0)
        mask_bias = jnp.where(sel == 1, 0.0, NEG)
        kk = kk_ref[0, pl.ds(ks, tk), :]

        def score(h):
            return lax.dot_general(kk, dqa_ref[0, :, h * HS:(h + 1) * HS], NT_DIMS,
                                   preferred_element_type=F32) + mask_bias

        _attend_tile(score, lambda h: vt_ref[0, j, (h // kv_rep) * VR:(h // kv_rep + 1) * VR, :],
                     m_scr, acc_scr, s_scr, tq)
        return carry

    lax.fori_loop(0, nkt, attend_tile, 0)
    _softmax_finish(o_ref, acc_scr)


def _dsa(dqa, iqa, iw_t, kk, vt, ik, *, t_len, kv_len):
    b, tp, _ = dqa.shape
    lp = kk.shape[1]
    tk = vt.shape[3]
    tq = min(256, tp)
    n_sel = min(TOPK_MAX, kv_len // 4)
    assert tk >= n_sel and lp % tk == 0 and tp % tq == 0
    nkt_max = lp // tk
    kern = functools.partial(_dsa_kernel, tq=tq, tk=tk, off=kv_len - t_len, kv_len=kv_len, n_sel=n_sel,
                             idx_bits=max(1, int(lp - 1).bit_length()))
    whole = lambda a: pl.BlockSpec((1,) + a.shape[1:], lambda bb, i: (bb,) + (0,) * (a.ndim - 1),
                                   pipeline_mode=pl.Buffered(1))
    return pl.pallas_call(
        kern,
        grid=(b, tp // tq),
        in_specs=[pl.BlockSpec((1, tq, QW), lambda bb, i: (bb, i, 0)),
                  pl.BlockSpec((1, tq, QW), lambda bb, i: (bb, i, 0)),
                  pl.BlockSpec((1, HEADS, tq), lambda bb, i: (bb, 0, i)),
                  whole(kk), whole(vt), whole(ik)],
        out_specs=pl.BlockSpec((1, tq, BRANCH_W), lambda bb, i: (bb, i, 0)),
        out_shape=jax.ShapeDtypeStruct((b, tp, BRANCH_W), F32),
        scratch_shapes=[pltpu.VMEM((nkt_max, tk, tq), jnp.int32), pltpu.VMEM((nkt_max, tk, tq), jnp.int16),
                        pltpu.VMEM((1, tq), jnp.int32),
                        pltpu.VMEM((1, HEADS * tq), F32), pltpu.VMEM((HEADS * VR, tq), F32),
                        pltpu.VMEM((tk, HEADS * tq), F32)],
        compiler_params=_cparams("parallel", "arbitrary"),
        name="dsa",
    )(dqa, iqa, iw_t, kk, vt, ik)


def _merge_kernel(x_ref, oa_ref, ob_ref, oc_ref, gz_ref, wa_ref, wb_ref, wc_ref, wo_ref, o_ref):
    d = x_ref.shape[1]
    merged = jnp.zeros(x_ref.shape, F32)
    for n, (o_br, w_br) in enumerate(((oa_ref, wa_ref), (ob_ref, wb_ref), (oc_ref, wc_ref))):
        gate = jax.nn.sigmoid(gz_ref[:, n * d:(n + 1) * d])
        merged = merged + gate * jnp.dot(o_br[...].astype(BF16), w_br[...], preferred_element_type=F32)
    o_ref[...] = x_ref[...] + jnp.dot(merged.astype(BF16), wo_ref[...], preferred_element_type=F32)


def _merge(x, oa, ob, oc, z, wa, wb, wc, wo):
    m, d = x.shape
    tm = _row_tile(m)
    row = lambda w: pl.BlockSpec((tm, w), lambda i: (i, 0))
    full = lambda a: pl.BlockSpec(a.shape, lambda i: (0, 0))
    return pl.pallas_call(
        _merge_kernel,
        grid=(m // tm,),
        in_specs=[row(d), row(BRANCH_W), row(BRANCH_W), row(BRANCH_W), row(3 * d), full(wa), full(wb), full(wc), full(wo)],
        out_specs=row(d),
        out_shape=jax.ShapeDtypeStruct((m, d), F32),
        compiler_params=_cparams("parallel"),
        name="merge",
    )(x, oa, ob, oc, z, wa, wb, wc, wo)


def _ffn_kernel(x_ref, g_ref, wg_ref, wu_ref, wd_ref, o_ref, hn_scr, acc_scr):
    j = pl.program_id(1)

    @pl.when(j == 0)
    def _():
        x = x_ref[...]
        hn_scr[...] = (x * lax.rsqrt(jnp.mean(x * x, axis=-1, keepdims=True) + EPS) * g_ref[...]).astype(BF16)
        acc_scr[...] = jnp.zeros(acc_scr.shape, F32)

    hn = hn_scr[...]
    a = jnp.dot(hn, wg_ref[...], preferred_element_type=F32)
    u = jnp.dot(hn, wu_ref[...], preferred_element_type=F32)
    acc_scr[...] += jnp.dot((jax.nn.silu(a) * u).astype(BF16), wd_ref[...], preferred_element_type=F32)

    @pl.when(j == pl.num_programs(1) - 1)
    def _():
        o_ref[...] = x_ref[...] + acc_scr[...]


def _ffn(x, g, wg, wu, wd):
    m, d = x.shape
    dff = wg.shape[1]
    tm = _row_tile(m)
    tf = dff // 2 if (dff // 2) % LANES == 0 else dff
    return pl.pallas_call(
        _ffn_kernel,
        grid=(m // tm, dff // tf),
        in_specs=[pl.BlockSpec((tm, d), lambda i, j: (i, 0)),
                  pl.BlockSpec((1, d), lambda i, j: (0, 0)),
                  pl.BlockSpec((d, tf), lambda i, j: (0, j)),
                  pl.BlockSpec((d, tf), lambda i, j: (0, j)),
                  pl.BlockSpec((tf, d), lambda i, j: (j, 0))],
        out_specs=pl.BlockSpec((tm, d), lambda i, j: (i, 0)),
        out_shape=jax.ShapeDtypeStruct((m, d), F32),
        scratch_shapes=[pltpu.VMEM((tm, d), BF16), pltpu.VMEM((tm, d), F32)],
        compiler_params=_cparams("parallel", "arbitrary"),
        name="ffn",
    )(x, g.reshape(1, d).astype(F32), wg, wu, wd)


def _rmsnorm_kernel(x_ref, g_ref, o_ref):
    x = x_ref[...]
    o_ref[...] = x * lax.rsqrt(jnp.mean(x * x, axis=-1, keepdims=True) + EPS) * g_ref[...]


def _rmsnorm(x, g):
    m, d = x.shape
    tm = _row_tile(m)
    return pl.pallas_call(
        _rmsnorm_kernel,
        grid=(m // tm,),
        in_specs=[pl.BlockSpec((tm, d), lambda i: (i, 0)), pl.BlockSpec((1, d), lambda i: (0, 0))],
        out_specs=pl.BlockSpec((tm, d), lambda i: (i, 0)),
        out_shape=jax.ShapeDtypeStruct((m, d), F32),
        compiler_params=_cparams("parallel"),
        name="final_rmsnorm",
    )(x, g.reshape(1, d).astype(F32))


def _head_strided(w, lane_of_head=lambda h: 0):
    lead = w.shape[:-1]
    w = w.reshape(lead + (HEADS, HEAD_DIM))
    zeros = jnp.zeros(lead + (HS - HEAD_DIM,), w.dtype)
    parts = []
    for h in range(HEADS):
        o = lane_of_head(h)
        assert o in (0, HS - HEAD_DIM)
        parts += [w[..., h, :], zeros] if o == 0 else [zeros, w[..., h, :]]
    return jnp.concatenate(parts, axis=-1)


def _pack_weights(w_in, w_mla_uq, w_mla_ukv):
    sizes = (BRANCH_W, BRANCH_W, BRANCH_W, HEADS, 256, 128, MLA_ROPE, BRANCH_W, 128, 128, BRANCH_W, HEAD_DIM, HEADS)
    offs = np.concatenate([[0], np.cumsum(sizes)])
    (fq, fk, fv, ff, cq, ckv, kr, dq, dk, dv, iq, ik, iw) = [w_in[..., int(offs[n]):int(offs[n + 1])]
                                                            for n in range(len(sizes))]
    gz = w_in[..., int(offs[-1]):]
    pad = jnp.zeros(w_in.shape[:-1] + (LANES - (HEAD_DIM + MLA_ROPE + 2 * HEADS),), w_in.dtype)
    kv_rep = HEADS // DSA_KV_HEADS
    w_in_p = jnp.concatenate(
        [gz, _head_strided(fq), _head_strided(fk), _head_strided(dq, lambda h: (h // kv_rep) * HEAD_DIM),
         _head_strided(iq), fv, fk, cq, ckv, dk, dv, ik, kr, ff, iw, pad], axis=-1).astype(BF16)
    assert w_in_p.shape[-1] == N_PACK

    depth = w_in.shape[0]
    uq = w_mla_uq.reshape(depth, w_mla_uq.shape[1], HEADS, HEAD_DIM + MLA_ROPE)
    uq = jnp.concatenate([uq, jnp.zeros(uq.shape[:-1] + (HS - HEAD_DIM - MLA_ROPE,), uq.dtype)], axis=-1)
    w_uq_p = uq.reshape(depth, -1, QW).astype(BF16)
    ukv = w_mla_ukv.reshape(depth, w_mla_ukv.shape[1], HEADS, 2 * HEAD_DIM)
    w_uk_p = _head_strided(ukv[..., :HEAD_DIM].reshape(depth, -1, BRANCH_W)).astype(BF16)
    w_uv_p = ukv[..., HEAD_DIM:].reshape(depth, -1, BRANCH_W).astype(BF16)
    return w_in_p, w_uq_p, w_uk_p, w_uv_p


def _with_past(past, new, lp):
    parts = [new] if past is None else [past.astype(new.dtype), new]
    rows = sum(p.shape[1] for p in parts)
    if lp > rows:
        parts.append(jnp.zeros((new.shape[0], lp - rows, new.shape[2]), new.dtype))
    return parts[0] if len(parts) == 1 else jnp.concatenate(parts, axis=1)


def _pad_axis(a, axis, size):
    if a.shape[axis] == size:
        return a
    widths = [(0, 0)] * a.ndim
    widths[axis] = (0, size - a.shape[axis])
    return jnp.pad(a, widths)


def _layer(x, past, tabs, p, layer, depth, bufs):
    b, t_len, d = x.shape
    m = b * t_len
    p_len = 0 if past is None else past[0].shape[1]
    kv_len = p_len + t_len
    tk = min(512, _round_up(kv_len, LANES))
    lp = _round_up(kv_len, tk)
    tp = _round_up(t_len, LANES)
    x2 = x.reshape(m, d)

    z = _rms_matmul(x2, 0, d, p["norm_attn"], p["w_in"], tn=N_PACK // 7)
    qm = _rms_matmul(z, C_CQ // 256, 256, p["norm_mla_q"], p["w_uq"], tn=256)
    fqa, dqa, iqa, mqa, ckv_n, dk_r, misc = _post(z, qm, tabs, p["b_forget"], p["norm_mla_kv"], t_len)

    z3 = z.reshape(b, t_len, N_PACK)
    r3 = lambda a: a.reshape(b, t_len, a.shape[-1])
    bufs = _write_state(layer, depth, bufs, z, ckv_n, dk_r, misc)
    fv = z3[..., C_FV:C_FV + BRANCH_W]
    dv = z3[..., C_DV:C_DV + LANES]
    misc3, ckv3, dk3 = r3(misc), r3(ckv_n), r3(dk_r)
    logf = misc3[..., M_FF:M_FF + HEADS]

    if past is None:
        fox_k, fox_v = (z3, C_FK // QW, QW), (z3, C_FV // BRANCH_W, BRANCH_W)
        dsa_k, dsa_v = (dk3, 0, LANES), (z3, C_DV // LANES, LANES)
        logf_all, ckv_all, kpe_all = logf, ckv3, misc3
        ik_all = misc3.astype(BF16)
    else:
        (c_fk, c_fv, c_logf, c_ckv, c_kpe, c_dk, c_dv, c_ik) = past
        flat = lambda a: a.reshape(b, p_len, -1)
        c_fk_s = _pad_axis(c_fk, 3, HS).reshape(b, p_len, QW)
        fox_k = (_with_past(c_fk_s, z3[..., C_FK:C_FK + QW], lp), 0, QW)
        fox_v = (_with_past(flat(c_fv), fv, lp), 0, BRANCH_W)
        dsa_k = (_with_past(flat(c_dk), dk3, lp), 0, LANES)
        dsa_v = (_with_past(flat(c_dv), dv, lp), 0, LANES)
        logf_all = _with_past(c_logf, logf, lp)
        ckv_all = _with_past(c_ckv, ckv3, lp)
        c_kpe_s = jnp.pad(c_kpe, ((0, 0), (0, 0), (M_KR, LANES - M_KR - MLA_ROPE)))
        kpe_all = _with_past(c_kpe_s, misc3, lp)
        ik_all = _with_past(_pad_axis(c_ik, 2, LANES), misc3, lp).astype(BF16)

    pad_q = lambda a: _pad_axis(a.reshape(b, t_len, a.shape[-1]), 1, tp)

    cum = _cumsum(_pad_axis(logf_all, 1, lp)) * LOG2E
    q_bias = _pad_axis(cum[:, :, kv_len - t_len:kv_len], 2, tp)
    fka, fvt = _kside(fox_k, fox_v, tk)
    o_a = _flash(pad_q(fqa), fka, fvt, t_len=t_len, kv_len=kv_len, per_frame=True,
                 bias=(q_bias, cum.transpose(0, 2, 1)))

    mka, mvt = _mla_kv(_pad_axis(ckv_all, 1, lp), _pad_axis(kpe_all, 1, lp), p["w_uk"], p["w_uv"], tk)
    o_b = _flash(pad_q(mqa), mka, mvt, t_len=t_len, kv_len=kv_len, per_frame=False)

    dka, dvt = _kside(dsa_k, dsa_v, tk)
    iw_t = _pad_axis(misc3[..., M_IW:M_IW + HEADS].transpose(0, 2, 1), 2, tp)
    o_c = _dsa(pad_q(dqa), pad_q(iqa), iw_t, dka, dvt, _pad_axis(ik_all, 1, lp), t_len=t_len, kv_len=kv_len)

    unpad = lambda o: o[:, :t_len].reshape(m, BRANCH_W)
    x2 = _merge(x2, unpad(o_a), unpad(o_b), unpad(o_c), z,
                p["w_br_fox"], p["w_br_mla"], p["w_br_dsa"], p["w_out"])
    x2 = _ffn(x2, p["norm_ffn"], p["w_ffn_gate"], p["w_ffn_up"], p["w_ffn_down"])
    return x2.reshape(b, t_len, d), bufs


def kernel(x_prompt, x_sample, cache_fox_k, cache_fox_v, cache_fox_logf, cache_mla_ckv, cache_mla_kpe, cache_dsa_k, cache_dsa_v, cache_dsa_idxk, norm_attn, w_in, b_forget, norm_mla_q, norm_mla_kv, w_mla_uq, w_mla_ukv, w_br_fox, w_br_mla, w_br_dsa, w_out, norm_ffn, w_ffn_gate, w_ffn_up, w_ffn_down, norm_final):
    depth = w_in.shape[0]
    past_len = cache_fox_k.shape[2]
    w_in_p, w_uq_p, w_uk_p, w_uv_p = _pack_weights(w_in, w_mla_uq, w_mla_ukv)
    bf = lambda a: a.astype(BF16)
    w_br_fox, w_br_mla, w_br_dsa, w_out = bf(w_br_fox), bf(w_br_mla), bf(w_br_dsa), bf(w_out)
    w_ffn_gate, w_ffn_up, w_ffn_down = bf(w_ffn_gate), bf(w_ffn_up), bf(w_ffn_down)
    caches = (cache_fox_k, cache_fox_v, cache_fox_logf, cache_mla_ckv, cache_mla_kpe,
              cache_dsa_k, cache_dsa_v, cache_dsa_idxk)

    tabs_p = _rope_tables(jnp.arange(x_prompt.shape[1]))
    tabs_s = _rope_tables(past_len + jnp.arange(x_sample.shape[1]))

    xp, xs = x_prompt, x_sample
    bufs_p = bufs_s = None
    for l in range(depth):
        p = dict(norm_attn=norm_attn[l], w_in=w_in_p[l], b_forget=b_forget[l], norm_mla_q=norm_mla_q[l],
                 norm_mla_kv=norm_mla_kv[l], w_uq=w_uq_p[l], w_uk=w_uk_p[l], w_uv=w_uv_p[l], w_br_fox=w_br_fox[l],
                 w_br_mla=w_br_mla[l], w_br_dsa=w_br_dsa[l], w_out=w_out[l], norm_ffn=norm_ffn[l],
                 w_ffn_gate=w_ffn_gate[l], w_ffn_up=w_ffn_up[l], w_ffn_down=w_ffn_down[l])
        xp, bufs_p = _layer(xp, None, tabs_p, p, l, depth, bufs_p)
        xs, bufs_s = _layer(xs, tuple(c[l] for c in caches), tabs_s, p, l, depth, bufs_s)

    def unflatten(bufs, like):
        bt = like.shape[:2]
        tails = ((HEADS, HEAD_DIM), (HEADS, HEAD_DIM), (HEADS,), (LANES,), (MLA_ROPE,),
                 (DSA_KV_HEADS, HEAD_DIM), (DSA_KV_HEADS, HEAD_DIM), (HEAD_DIM,))
        return [a.reshape((depth,) + bt + tail) for a, tail in zip(bufs, tails)]

    stacked_p = unflatten(bufs_p, x_prompt)
    stacked_s = unflatten(bufs_s, x_sample)
    yp = _rmsnorm(xp.reshape(-1, xp.shape[-1]), norm_final).reshape(xp.shape)
    ys = _rmsnorm(xs.reshape(-1, xs.shape[-1]), norm_final).reshape(xs.shape)
    out = [yp, ys]
    for sp, ss in zip(stacked_p, stacked_s):
        out += [sp, ss]
    return tuple(out)
```
